```python
import jax, jax.numpy as jnp
from jax import lax
import numpy as np

D_MODEL = 1024
BATCH = 8
SEQ = 8192
DEPTH = 2

N_MIXERS = 2
EPS = 1e-6

A_HEADS = 8
A_HEAD_DIM = D_MODEL // A_HEADS
A_KEY_DIM = A_HEADS * A_HEAD_DIM
A_VAL_DIM = A_HEADS * A_HEAD_DIM
A_QKV_DIM = 2 * A_KEY_DIM + A_VAL_DIM
A_IN_DIM = A_QKV_DIM + A_VAL_DIM + 2 * A_HEADS
CONV_WIDTH = 4
CHUNK = 64

B_HEADS = 8
B_HEAD_DIM = D_MODEL // B_HEADS
B_WIDTH = B_HEADS * B_HEAD_DIM
B_IN_DIM = 4 * B_WIDTH
Q_BLOCK = 128

N_A_LAYERS = (DEPTH + 1) // 2
N_B_LAYERS = DEPTH // 2

kernel_name = 'hybrid_gated_deltanet_stick_breaking'


def rmsnorm(x, w):
    xf = x.astype(jnp.float32)
    y = xf * lax.rsqrt(jnp.mean(xf * xf, axis=-1, keepdims=True) + EPS) * w.astype(jnp.float32)
    return y.astype(x.dtype)


def l2norm(x):
    xf = x.astype(jnp.float32)
    return xf * lax.rsqrt(jnp.sum(xf * xf, axis=-1, keepdims=True) + EPS)


def causal_depthwise_conv(x, w):
    K, C = w.shape
    return lax.conv_general_dilated(
        x, w[:, None, :].astype(x.dtype), window_strides=(1,), padding=[(K - 1, 0)],
        dimension_numbers=('NWC', 'WIO', 'NWC'), feature_group_count=C)


def chunk_gated_delta_rule(q, k, v, g, beta):
    Bsz, T, H, Dk = q.shape
    Dv = v.shape[-1]
    N = T // CHUNK

    def to_chunks(a):
        a = a.reshape((Bsz, N, CHUNK, H) + a.shape[3:])
        return jnp.moveaxis(a, 3, 2)

    q, k, v, g, beta = map(to_chunks, (q, k, v, g, beta))
    g = jnp.cumsum(g, axis=-1)
    idx = jnp.arange(CHUNK)
    causal = idx[:, None] >= idx[None, :]
    strict = idx[:, None] > idx[None, :]
    decay = jnp.exp(jnp.where(causal, g[..., :, None] - g[..., None, :], -jnp.inf))

    kb = k * beta[..., None]
    vb = v * beta[..., None]
    lower = jnp.where(strict, jnp.einsum('bnhik,bnhjk->bnhij', kb, k) * decay, 0.0)
    tri = lower + jnp.eye(CHUNK, dtype=jnp.float32)
    rhs = jnp.concatenate([vb, kb * jnp.exp(g)[..., None]], axis=-1)
    sol = lax.linalg.triangular_solve(tri, rhs, left_side=True, lower=True)
    u = sol[..., :Dv]
    w = sol[..., Dv:]

    attn_intra = jnp.einsum('bnhik,bnhjk->bnhij', q, k) * decay
    q_g = q * jnp.exp(g)[..., None]
    k_tail = k * jnp.exp(g[..., -1:] - g)[..., None]
    g_last = jnp.exp(g[..., -1])

    xs = tuple(jnp.moveaxis(a, 1, 0) for a in (u, w, attn_intra, q_g, k_tail, g_last))

    def step(S, inp):
        u_i, w_i, a_i, qg_i, kt_i, gl_i = inp
        v_new = u_i - jnp.einsum('bhck,bhkv->bhcv', w_i, S)
        o_i = jnp.einsum('bhck,bhkv->bhcv', qg_i, S) + jnp.einsum('bhij,bhjv->bhiv', a_i, v_new)
        S = S * gl_i[..., None, None] + jnp.einsum('bhck,bhcv->bhkv', kt_i, v_new)
        return S, o_i

    S0 = jnp.zeros((Bsz, H, Dk, Dv), jnp.float32)
    _, o = lax.scan(step, S0, xs)
    return o.transpose(1, 0, 3, 2, 4).reshape(Bsz, T, H, Dv)


def gated_deltanet_branch(h, w_in, conv_w, a_log, dt_bias, o_norm, w_out):
    Bsz, T, _ = h.shape
    proj = h @ w_in
    qkv, z, b_logit, a_logit = jnp.split(
        proj, [A_QKV_DIM, A_QKV_DIM + A_VAL_DIM, A_QKV_DIM + A_VAL_DIM + A_HEADS], axis=-1)
    qkv = jax.nn.silu(causal_depthwise_conv(qkv, conv_w))
    q, k, v = jnp.split(qkv, [A_KEY_DIM, 2 * A_KEY_DIM], axis=-1)
    q = l2norm(q.reshape(Bsz, T, A_HEADS, A_HEAD_DIM)) * (A_HEAD_DIM ** -0.5)
    k = l2norm(k.reshape(Bsz, T, A_HEADS, A_HEAD_DIM))
    v = v.reshape(Bsz, T, A_HEADS, A_HEAD_DIM).astype(jnp.float32)
    beta = jax.nn.sigmoid(b_logit.astype(jnp.float32))
    g = -jnp.exp(a_log.astype(jnp.float32)) * jax.nn.softplus(
        a_logit.astype(jnp.float32) + dt_bias.astype(jnp.float32))
    o = chunk_gated_delta_rule(q, k, v, g, beta)
    o = rmsnorm(o, o_norm) * jax.nn.silu(z.reshape(Bsz, T, A_HEADS, A_HEAD_DIM).astype(jnp.float32))
    return o.reshape(Bsz, T, A_VAL_DIM).astype(h.dtype) @ w_out


def stick_breaking_branch(h, w_in, w_out):
    Bsz, T, _ = h.shape
    proj = h @ w_in
    q, k, v, gate = jnp.split(proj, 4, axis=-1)

    def heads(t):
        return t.reshape(Bsz, T, B_HEADS, B_HEAD_DIM).transpose(0, 2, 1, 3)

    q, k, v = heads(q), heads(k), heads(v)
    nb = T // Q_BLOCK
    q_blocks = q.reshape(Bsz, B_HEADS, nb, Q_BLOCK, B_HEAD_DIM).transpose(2, 0, 1, 3, 4)
    key_pos = jnp.arange(T)
    scale = B_HEAD_DIM ** -0.5

    def block(args):
        q_blk, start = args
        qpos = start + jnp.arange(Q_BLOCK)
        mask = key_pos[None, :] < qpos[:, None]
        z = jnp.einsum('bhqd,bhsd->bhqs', q_blk, k).astype(jnp.float32) * scale
        log_1m = jnp.where(mask, jax.nn.log_sigmoid(-z), 0.0)
        tail = lax.cumsum(log_1m, axis=3, reverse=True) - log_1m
        A = jnp.where(mask, jnp.exp(jax.nn.log_sigmoid(z) + tail), 0.0)
        return jnp.einsum('bhqs,bhsd->bhqd', A.astype(v.dtype), v)

    o = lax.map(block, (q_blocks, jnp.arange(nb) * Q_BLOCK))
    o = o.transpose(1, 0, 3, 2, 4).reshape(Bsz, T, B_WIDTH)
    return (o * jax.nn.silu(gate)) @ w_out


def setup_inputs(seed: int = 0) -> dict:
    key = jax.random.key(seed)
    ks = jax.random.split(key, 14)
    f32 = jnp.float32
    x = jax.random.normal(ks[0], (BATCH, SEQ, D_MODEL), f32)
    norm_w = 1.0 + 0.02 * jax.random.normal(ks[1], (DEPTH, D_MODEL), f32)
    a_w_in = jax.random.normal(ks[2], (N_A_LAYERS, D_MODEL, A_IN_DIM), f32) * D_MODEL ** -0.5
    a_conv_w = jax.random.normal(ks[3], (N_A_LAYERS, CONV_WIDTH, A_QKV_DIM), f32) * CONV_WIDTH ** -0.5
    a_a_log = jnp.log(jax.random.uniform(ks[4], (N_A_LAYERS, A_HEADS), f32, 1.0, 16.0))
    dt = jnp.exp(jax.random.uniform(ks[5], (N_A_LAYERS, A_HEADS), f32, np.log(1e-3), np.log(1e-1)))
    a_dt_bias = dt + jnp.log(-jnp.expm1(-dt))
    a_o_norm = 1.0 + 0.02 * jax.random.normal(ks[6], (N_A_LAYERS, A_HEAD_DIM), f32)
    a_w_out = jax.random.normal(ks[7], (N_A_LAYERS, A_VAL_DIM, D_MODEL), f32) * A_VAL_DIM ** -0.5
    b_w_in = jax.random.normal(ks[8], (N_B_LAYERS, D_MODEL, B_IN_DIM), f32) * D_MODEL ** -0.5
    b_w_out = jax.random.normal(ks[9], (N_B_LAYERS, B_WIDTH, D_MODEL), f32) * B_WIDTH ** -0.5
    final_norm_w = 1.0 + 0.02 * jax.random.normal(ks[10], (D_MODEL,), f32)
    return {'x': x, 'norm_w': norm_w, 'a_w_in': a_w_in, 'a_conv_w': a_conv_w, 'a_a_log': a_a_log,
            'a_dt_bias': a_dt_bias, 'a_o_norm': a_o_norm, 'a_w_out': a_w_out,
            'b_w_in': b_w_in, 'b_w_out': b_w_out, 'final_norm_w': final_norm_w}


def reference(x, norm_w, a_w_in, a_conv_w, a_a_log, a_dt_bias, a_o_norm, a_w_out, b_w_in, b_w_out, final_norm_w):
    h = x
    for i in range(DEPTH):
        u = rmsnorm(h, norm_w[i])
        j = i // N_MIXERS
        if i % N_MIXERS == 0:
            h = h + gated_deltanet_branch(u, a_w_in[j], a_conv_w[j], a_a_log[j], a_dt_bias[j],
                                          a_o_norm[j], a_w_out[j])
        else:
            h = h + stick_breaking_branch(u, b_w_in[j], b_w_out[j])
    return rmsnorm(h, final_norm_w)
```

```python
import functools

import jax
import jax.numpy as jnp
from jax import lax
from jax.experimental import pallas as pl
from jax.experimental.pallas import tpu as pltpu

EPS = 1e-6
HEADS = 8
HEAD_DIM = 128
WIDTH = HEADS * HEAD_DIM
CHUNK = 64
CONV_WIDTH = 4
LANES = 128
SUBLANES = 8
GDN_BLOCK = 256
SB_BLOCK = 256
SB_SUB = 128
VMEM_LIMIT = 48 * 1024 * 1024

F32 = jnp.float32
BF16 = jnp.bfloat16


def _sigmoid(x):
    return 1.0 / (1.0 + jnp.exp(-x))


def _softplus(x):
    return jnp.maximum(x, 0.0) + jnp.log(1.0 + jnp.exp(-jnp.abs(x)))


def _dot(a, b):
    return jnp.dot(a, b, preferred_element_type=F32)


def _dot_nt(a, b):
    return lax.dot_general(a, b, (((1,), (1,)), ((), ())), preferred_element_type=F32)


def _norm_proj_kernel(x_ref, nw_ref, w_ref, o_ref, xn_ref):
    @pl.when(pl.program_id(1) == 0)
    def _():
        x = x_ref[...]
        ms = jnp.mean(x * x, axis=-1, keepdims=True)
        xn_ref[...] = (x * lax.rsqrt(ms + EPS) * nw_ref[...]).astype(BF16)

    o_ref[...] = _dot(xn_ref[...], w_ref[...]).astype(o_ref.dtype)


def _norm_proj(x2, nw, w, out_dtype, tm, tn):
    m, d = x2.shape
    n = w.shape[1]
    return pl.pallas_call(
        _norm_proj_kernel,
        grid=(m // tm, n // tn),
        in_specs=[
            pl.BlockSpec((tm, d), lambda i, j: (i, 0)),
            pl.BlockSpec((1, d), lambda i, j: (0, 0)),
            pl.BlockSpec((d, tn), lambda i, j: (0, j)),
        ],
        out_specs=pl.BlockSpec((tm, tn), lambda i, j: (i, j)),
        out_shape=jax.ShapeDtypeStruct((m, n), out_dtype),
        scratch_shapes=[pltpu.VMEM((tm, d), BF16)],
        compiler_params=pltpu.CompilerParams(
            dimension_semantics=("parallel", "arbitrary"), vmem_limit_bytes=VMEM_LIMIT),
        name="norm_proj",
    )(x2, nw.reshape(1, d), w)


def _out_proj_kernel(a_ref, w_ref, r_ref, fw_ref, o_ref, *, final_norm):
    y = r_ref[...] + _dot(a_ref[...].astype(BF16), w_ref[...])
    if final_norm:
        ms = jnp.mean(y * y, axis=-1, keepdims=True)
        y = y * lax.rsqrt(ms + EPS) * fw_ref[...]
    o_ref[...] = y


def _out_proj(a2, w, res2, fw, final_norm, tm):
    m, d = a2.shape
    n = w.shape[1]
    return pl.pallas_call(
        functools.partial(_out_proj_kernel, final_norm=final_norm),
        grid=(m // tm,),
        in_specs=[
            pl.BlockSpec((tm, d), lambda i: (i, 0)),
            pl.BlockSpec((d, n), lambda i: (0, 0)),
            pl.BlockSpec((tm, n), lambda i: (i, 0)),
            pl.BlockSpec((1, n), lambda i: (0, 0)),
        ],
        out_specs=pl.BlockSpec((tm, n), lambda i: (i, 0)),
        out_shape=jax.ShapeDtypeStruct((m, n), F32),
        compiler_params=pltpu.CompilerParams(
            dimension_semantics=("parallel",), vmem_limit_bytes=VMEM_LIMIT),
        name="out_proj",
    )(a2, w, res2, fw.reshape(1, n))


def _col(x, lane_idx, lane):
    return jnp.sum(jnp.where(lane == lane_idx, x, 0.0), axis=-1, keepdims=True)


def _split3(x):
    h1 = x.astype(BF16)
    r1 = x - h1.astype(F32)
    h2 = r1.astype(BF16)
    h3 = (r1 - h2.astype(F32)).astype(BF16)
    return h1, h2, h3


def _gdn_kernel(q_ref, k_ref, v_ref, z_ref, bg_ref, cw_ref, gp_ref, onw_ref, cs_ref,
                o_ref, tail_ref, s_ref, qn_ref, kn_ref, vn_ref, gc_ref, beta_ref):
    tb = q_ref.shape[1]
    n_chunks = tb // CHUNK

    @pl.when(pl.program_id(1) == 0)
    def _():
        tail_ref[...] = jnp.zeros_like(tail_ref)
        s_ref[...] = jnp.zeros_like(s_ref)

    bg = bg_ref[0]
    beta_ref[...] = _sigmoid(bg)
    g = -jnp.exp(gp_ref[0:1, :]) * _softplus(bg + gp_ref[1:2, :])
    cs = cs_ref[...]
    g1, g2, g3 = _split3(g)
    gc_ref[...] = _dot(cs, g1) + _dot(cs, g2) + _dot(cs, g3)

    scale = HEAD_DIM ** -0.5
    for a, (src, dst) in enumerate(((q_ref, qn_ref), (k_ref, kn_ref), (v_ref, vn_ref))):
        for h in range(HEADS):
            cols = slice(h * HEAD_DIM, (h + 1) * HEAD_DIM)
            wcols = slice(a * WIDTH + h * HEAD_DIM, a * WIDTH + (h + 1) * HEAD_DIM)
            x = src[0, :, cols]
            xp = jnp.concatenate([tail_ref[a, :, cols], x], axis=0)
            y = x * cw_ref[3:4, wcols]
            for s in range(1, CONV_WIDTH):
                shifted = pltpu.roll(xp, s, axis=0)[SUBLANES:]
                y = y + shifted * cw_ref[3 - s:4 - s, wcols]
            tail_ref[a, :, cols] = x[tb - SUBLANES:]
            y = y * _sigmoid(y)
            if a < 2:
                y = y * lax.rsqrt(jnp.sum(y * y, axis=-1, keepdims=True) + EPS)
                if a == 0:
                    y = y * scale
            dst[:, cols] = y

    lane = lax.broadcasted_iota(jnp.int32, (CHUNK, LANES), 1)
    row = lax.broadcasted_iota(jnp.int32, (CHUNK, LANES), 0)
    lane_t = lane & (CHUNK - 1)
    left = lane < CHUNK
    causal = row >= lane_t
    w_init = jnp.where(left, 0.0, jnp.where(row == lane_t, 1.0, 0.0))
    strict_left = jnp.logical_and(left, row > lane_t)
    onw = onw_ref[...]

    def chunk_body(c, carry):
        r0 = pl.multiple_of(c * CHUNK, CHUNK)
        rows = pl.ds(r0, CHUNK)
        gc = gc_ref[rows, :]
        beta = beta_ref[rows, :]
        g_last = jnp.broadcast_to(gc[CHUNK - 1:CHUNK, :], (CHUNK, LANES))
        eg = jnp.exp(gc)
        et = jnp.exp(g_last - gc)
        egl = jnp.exp(g_last)
        gct = jnp.concatenate([gc, gc], axis=0).T

        for h in range(HEADS):
            cols = slice(h * HEAD_DIM, (h + 1) * HEAD_DIM)
            qh = qn_ref[rows, cols]
            kh = kn_ref[rows, cols]
            vh = vn_ref[rows, cols]
            b_col = _col(beta, h, lane)
            g_col = _col(gc, HEADS + h, lane)
            eg_col = _col(eg, HEADS + h, lane)
            et_col = _col(et, HEADS + h, lane)
            gl_col = _col(egl, HEADS + h, lane)
            g_row = gct[HEADS + h:HEADS + h + 1, :]

            decay = jnp.exp(jnp.where(causal, g_col - g_row, -jnp.inf))
            kb = kh * b_col
            kcat = jnp.concatenate([kh, kh], axis=0).astype(BF16)
            kk = _dot_nt(kb.astype(BF16), kcat)
            attn = _dot_nt(qh.astype(BF16), kcat) * decay

            wide = jnp.where(strict_left, -(kk * decay), w_init)
            for _ in range(6):
                x_left = jnp.where(left, wide, 0.0).astype(BF16)
                prod = _dot(x_left, jnp.concatenate([wide, wide], axis=0).astype(BF16))
                wide = prod + jnp.where(left, 0.0, wide)
            t_right = jnp.where(left, 0.0, wide).astype(BF16)

            rhs = jnp.concatenate([vh * b_col, kb * eg_col], axis=1)
            sol = _dot(t_right, jnp.concatenate([rhs, rhs], axis=0).astype(BF16))
            u = sol[:, :HEAD_DIM]
            w = sol[:, HEAD_DIM:]

            s_old = s_ref[h]
            lhs = jnp.concatenate([w, qh * eg_col], axis=0).astype(BF16)
            ws_qs = _dot(lhs, s_old.astype(BF16))
            v_new = u - ws_qs[:CHUNK]
            vn2 = jnp.concatenate([v_new, v_new], axis=0).astype(BF16)
            a_left = jnp.where(left, attn, 0.0).astype(BF16)
            o = ws_qs[CHUNK:] + _dot(a_left, vn2)

            kt = kh * et_col
            kt_t = jnp.concatenate([kt, jnp.zeros_like(kt)], axis=0).T
            gl2 = jnp.concatenate([gl_col, gl_col], axis=0)
            s_ref[h] = s_old * gl2 + _dot(kt_t.astype(BF16), vn2)

            on = o * lax.rsqrt(jnp.mean(o * o, axis=-1, keepdims=True) + EPS) * onw
            zh = z_ref[0, rows, cols]
            o_ref[0, rows, cols] = on * (zh * _sigmoid(zh))
        return carry

    lax.fori_loop(0, n_chunks, chunk_body, 0)


def _gdn_core(proj, conv_w, gate_par, o_norm_w, cs, bsz, seq):
    tb = GDN_BLOCK
    nq = WIDTH // WIDTH
    del nq
    return pl.pallas_call(
        _gdn_kernel,
        grid=(bsz, seq // tb),
        in_specs=[
            pl.BlockSpec((1, tb, WIDTH), lambda b, t: (b, t, 0)),
            pl.BlockSpec((1, tb, WIDTH), lambda b, t: (b, t, 1)),
            pl.BlockSpec((1, tb, WIDTH), lambda b, t: (b, t, 2)),
            pl.BlockSpec((1, tb, WIDTH), lambda b, t: (b, t, 3)),
            pl.BlockSpec((1, tb, LANES), lambda b, t: (b, t, 4 * WIDTH // LANES)),
            pl.BlockSpec((CONV_WIDTH, 3 * WIDTH), lambda b, t: (0, 0)),
            pl.BlockSpec((2, LANES), lambda b, t: (0, 0)),
            pl.BlockSpec((1, HEAD_DIM), lambda b, t: (0, 0)),
            pl.BlockSpec((tb, tb), lambda b, t: (0, 0)),
        ],
        out_specs=pl.BlockSpec((1, tb, WIDTH), lambda b, t: (b, t, 0)),
        out_shape=jax.ShapeDtypeStruct((bsz, seq, WIDTH), F32),
        scratch_shapes=[
            pltpu.VMEM((3, SUBLANES, WIDTH), F32),
            pltpu.VMEM((HEADS, HEAD_DIM, HEAD_DIM), F32),
            pltpu.VMEM((tb, WIDTH), F32),
            pltpu.VMEM((tb, WIDTH), F32),
            pltpu.VMEM((tb, WIDTH), F32),
            pltpu.VMEM((tb, LANES), F32),
            pltpu.VMEM((tb, LANES), F32),
        ],
        compiler_params=pltpu.CompilerParams(
            dimension_semantics=("parallel", "arbitrary"), vmem_limit_bytes=VMEM_LIMIT),
        name="gdn_core",
    )(proj, proj, proj, proj, proj, conv_w, gate_par, o_norm_w.reshape(1, HEAD_DIM), cs)


def _sb_kernel(q_ref, k_ref, v_ref, gate_ref, mo_ref, o_ref, acc_ref, c_ref):
    tq = q_ref.shape[1]
    qi = pl.program_id(2)
    q = q_ref[0]
    mo = mo_ref[...]
    scale = HEAD_DIM ** -0.5
    acc_ref[...] = jnp.zeros_like(acc_ref)
    c_ref[...] = jnp.zeros_like(c_ref)
    n_sub = tq // SB_SUB

    def group(kstart, masked):
        kblk = k_ref[0, pl.ds(kstart, tq), :]
        vblk = v_ref[0, pl.ds(kstart, tq), :]
        z = _dot_nt(q, kblk) * scale
        sp = _softplus(z)
        if masked:
            mask = (lax.broadcasted_iota(jnp.int32, (tq, tq), 0)
                    > lax.broadcasted_iota(jnp.int32, (tq, tq), 1))
            sp = jnp.where(mask, sp, 0.0)
        c = c_ref[...]
        probs = [None] * n_sub
        for jj in reversed(range(n_sub)):
            sl = slice(jj * SB_SUB, (jj + 1) * SB_SUB)
            spj = sp[:, sl]
            hi = spj.astype(BF16)
            lo = (spj - hi.astype(F32)).astype(BF16)
            t2 = _dot(jnp.concatenate([hi, lo], axis=1), mo)
            tail = c + t2[:, :SB_SUB]
            c = c + t2[:, SB_SUB:]
            a = jnp.exp(z[:, sl] - spj - tail)
            if masked:
                a = jnp.where(mask[:, sl], a, 0.0)
            probs[jj] = a.astype(BF16)
        c_ref[...] = c
        acc_ref[...] += _dot(jnp.concatenate(probs, axis=1), vblk)

    group(pl.multiple_of(qi * tq, tq), True)

    def body(it, carry):
        group(pl.multiple_of((qi - 1 - it) * tq, tq), False)
        return carry

    lax.fori_loop(0, qi, body, 0)
    gate = gate_ref[0]
    o_ref[0] = acc_ref[...] * (gate * _sigmoid(gate))


def _sb_attention(qkv, gate, mo, bsz, seq):
    tq = SB_BLOCK
    return pl.pallas_call(
        _sb_kernel,
        grid=(bsz, HEADS, seq // tq),
        in_specs=[
            pl.BlockSpec((1, tq, HEAD_DIM), lambda b, h, i: (b, i, h)),
            pl.BlockSpec((1, seq, HEAD_DIM), lambda b, h, i: (b, 0, HEADS + h)),
            pl.BlockSpec((1, seq, HEAD_DIM), lambda b, h, i: (b, 0, 2 * HEADS + h)),
            pl.BlockSpec((1, tq, HEAD_DIM), lambda b, h, i: (b, i, h)),
            pl.BlockSpec((2 * SB_SUB, 2 * SB_SUB), lambda b, h, i: (0, 0)),
        ],
        out_specs=pl.BlockSpec((1, tq, HEAD_DIM), lambda b, h, i: (b, i, h)),
        out_shape=jax.ShapeDtypeStruct((bsz, seq, WIDTH), F32),
        scratch_shapes=[
            pltpu.VMEM((tq, HEAD_DIM), F32),
            pltpu.VMEM((tq, SB_SUB), F32),
        ],
        compiler_params=pltpu.CompilerParams(
            dimension_semantics=("parallel", "parallel", "arbitrary"), vmem_limit_bytes=VMEM_LIMIT),
        name="sb_attention",
    )(qkv, qkv, qkv, gate, mo)


def _cumsum_matrix(tb):
    i = jnp.arange(tb)
    same_chunk = (i[:, None] // CHUNK) == (i[None, :] // CHUNK)
    return jnp.logical_and(same_chunk, i[:, None] >= i[None, :]).astype(BF16)


def _tail_matrix():
    j = jnp.arange(2 * SB_SUB)[:, None] % SB_SUB
    s = jnp.arange(2 * SB_SUB)[None, :]
    return jnp.where(s < SB_SUB, j > s, True).astype(BF16)


def kernel(x, norm_w, a_w_in, a_conv_w, a_a_log, a_dt_bias, a_o_norm, a_w_out, b_w_in, b_w_out, final_norm_w):
    bsz, seq, d = x.shape
    m = bsz * seq
    depth = norm_w.shape[0]
    tm = min(1024, m)
    h2 = x.reshape(m, d)
    for i in range(depth):
        j = i // 2
        last = i == depth - 1
        if i % 2 == 0:
            wa = a_w_in[j]
            pad = jnp.zeros((d, LANES - 2 * HEADS), F32)
            w_all = jnp.concatenate([wa, pad], axis=1).astype(BF16)
            proj = _norm_proj(h2, norm_w[i], w_all, F32, tm, w_all.shape[1] // 3)
            gate_par = jnp.zeros((2, LANES), F32)
            gate_par = gate_par.at[0, HEADS:2 * HEADS].set(a_a_log[j])
            gate_par = gate_par.at[1, HEADS:2 * HEADS].set(a_dt_bias[j])
            og = _gdn_core(proj.reshape(bsz, seq, -1), a_conv_w[j], gate_par, a_o_norm[j],
                           _cumsum_matrix(GDN_BLOCK), bsz, seq)
            w_out = a_w_out[j]
        else:
            wb = b_w_in[j]
            qkv = _norm_proj(h2, norm_w[i], wb[:, :3 * WIDTH].astype(BF16), BF16, tm, WIDTH)
            gate = _norm_proj(h2, norm_w[i], wb[:, 3 * WIDTH:].astype(BF16), F32, tm, WIDTH)
            og = _sb_attention(qkv.reshape(bsz, seq, -1), gate.reshape(bsz, seq, -1),
                               _tail_matrix(), bsz, seq)
            w_out = b_w_out[j]
        h2 = _out_proj(og.reshape(m, WIDTH), w_out.astype(BF16), h2, final_norm_w, last, min(512, m))
    return h2.reshape(bsz, seq, d)
```

```python
import functools

import jax
import jax.numpy as jnp
from jax import lax
from jax.experimental import pallas as pl
from jax.experimental.pallas import tpu as pltpu

EPS = 1e-6
HEADS = 8
HEAD_DIM = 128
WIDTH = HEADS * HEAD_DIM
CHUNK = 64
CONV_WIDTH = 4
LANES = 128
SUBLANES = 8
GDN_BLOCK = 256
SB_BLOCK = 512
LOG2E = 1.4426950408889634
SB_SUB = 128
VMEM_LIMIT = 48 * 1024 * 1024

F32 = jnp.float32
BF16 = jnp.bfloat16


def _sigmoid(x):
    return 1.0 / (1.0 + jnp.exp(-x))


def _softplus(x):
    return jnp.maximum(x, 0.0) + jnp.log(1.0 + jnp.exp(-jnp.abs(x)))


def _dot(a, b):
    return jnp.dot(a, b, preferred_element_type=F32)


def _dot_nt(a, b):
    return lax.dot_general(a, b, (((1,), (1,)), ((), ())), preferred_element_type=F32)


def _norm_proj_kernel(x_ref, nw_ref, w_ref, o_ref, xn_ref):
    @pl.when(pl.program_id(1) == 0)
    def _():
        x = x_ref[...]
        ms = jnp.mean(x * x, axis=-1, keepdims=True)
        xn_ref[...] = (x * lax.rsqrt(ms + EPS) * nw_ref[...]).astype(BF16)

    o_ref[...] = _dot(xn_ref[...], w_ref[...]).astype(o_ref.dtype)


def _norm_proj(x2, nw, w, out_dtype, tm, tn):
    m, d = x2.shape
    n = w.shape[1]
    return pl.pallas_call(
        _norm_proj_kernel,
        grid=(m // tm, n // tn),
        in_specs=[
            pl.BlockSpec((tm, d), lambda i, j: (i, 0)),
            pl.BlockSpec((1, d), lambda i, j: (0, 0)),
            pl.BlockSpec((d, tn), lambda i, j: (0, j)),
        ],
        out_specs=pl.BlockSpec((tm, tn), lambda i, j: (i, j)),
        out_shape=jax.ShapeDtypeStruct((m, n), out_dtype),
        scratch_shapes=[pltpu.VMEM((tm, d), BF16)],
        compiler_params=pltpu.CompilerParams(
            dimension_semantics=("parallel", "arbitrary"), vmem_limit_bytes=VMEM_LIMIT),
        name="norm_proj",
    )(x2, nw.reshape(1, d), w)


def _out_proj_kernel(a_ref, w_ref, r_ref, fw_ref, o_ref, *, final_norm):
    y = r_ref[...] + _dot(a_ref[...].astype(BF16), w_ref[...])
    if final_norm:
        ms = jnp.mean(y * y, axis=-1, keepdims=True)
        y = y * lax.rsqrt(ms + EPS) * fw_ref[...]
    o_ref[...] = y


def _out_proj(a2, w, res2, fw, final_norm, tm):
    m, d = a2.shape
    n = w.shape[1]
    return pl.pallas_call(
        functools.partial(_out_proj_kernel, final_norm=final_norm),
        grid=(m // tm,),
        in_specs=[
            pl.BlockSpec((tm, d), lambda i: (i, 0)),
            pl.BlockSpec((d, n), lambda i: (0, 0)),
            pl.BlockSpec((tm, n), lambda i: (i, 0)),
            pl.BlockSpec((1, n), lambda i: (0, 0)),
        ],
        out_specs=pl.BlockSpec((tm, n), lambda i: (i, 0)),
        out_shape=jax.ShapeDtypeStruct((m, n), F32),
        compiler_params=pltpu.CompilerParams(
            dimension_semantics=("parallel",), vmem_limit_bytes=VMEM_LIMIT),
        name="out_proj",
    )(a2, w, res2, fw.reshape(1, n))


def _col(x, lane_idx, lane):
    return jnp.sum(jnp.where(lane == lane_idx, x, 0.0), axis=-1, keepdims=True)


def _split3(x):
    h1 = x.astype(BF16)
    r1 = x - h1.astype(F32)
    h2 = r1.astype(BF16)
    h3 = (r1 - h2.astype(F32)).astype(BF16)
    return h1, h2, h3


def _gdn_kernel(q_ref, k_ref, v_ref, z_ref, bg_ref, cw_ref, gp_ref, onw_ref, cs_ref,
                o_ref, tail_ref, s_ref, qn_ref, kn_ref, vn_ref, gc_ref, beta_ref):
    tb = q_ref.shape[1]
    n_chunks = tb // CHUNK

    @pl.when(pl.program_id(1) == 0)
    def _():
        tail_ref[...] = jnp.zeros_like(tail_ref)
        s_ref[...] = jnp.zeros_like(s_ref)

    bg = bg_ref[0]
    beta_ref[...] = _sigmoid(bg)
    g = -jnp.exp(gp_ref[0:1, :]) * _softplus(bg + gp_ref[1:2, :])
    cs = cs_ref[...]
    g1, g2, g3 = _split3(g)
    gc_ref[...] = _dot(cs, g1) + _dot(cs, g2) + _dot(cs, g3)

    scale = HEAD_DIM ** -0.5
    for a, (src, dst) in enumerate(((q_ref, qn_ref), (k_ref, kn_ref), (v_ref, vn_ref))):
        for h in range(HEADS):
            cols = slice(h * HEAD_DIM, (h + 1) * HEAD_DIM)
            wcols = slice(a * WIDTH + h * HEAD_DIM, a * WIDTH + (h + 1) * HEAD_DIM)
            x = src[0, :, cols]
            xp = jnp.concatenate([tail_ref[a, :, cols], x], axis=0)
            y = x * cw_ref[3:4, wcols]
            for s in range(1, CONV_WIDTH):
                shifted = pltpu.roll(xp, s, axis=0)[SUBLANES:]
                y = y + shifted * cw_ref[3 - s:4 - s, wcols]
            tail_ref[a, :, cols] = x[tb - SUBLANES:]
            y = y * _sigmoid(y)
            if a < 2:
                y = y * lax.rsqrt(jnp.sum(y * y, axis=-1, keepdims=True) + EPS)
                if a == 0:
                    y = y * scale
            dst[:, cols] = y

    lane = lax.broadcasted_iota(jnp.int32, (CHUNK, LANES), 1)
    row = lax.broadcasted_iota(jnp.int32, (CHUNK, LANES), 0)
    lane_t = lane & (CHUNK - 1)
    left = lane < CHUNK
    causal = row >= lane_t
    w_init = jnp.where(left, 0.0, jnp.where(row == lane_t, 1.0, 0.0))
    strict_left = jnp.logical_and(left, row > lane_t)
    onw = onw_ref[...]

    def chunk_body(c, carry):
        r0 = pl.multiple_of(c * CHUNK, CHUNK)
        rows = pl.ds(r0, CHUNK)
        gc = gc_ref[rows, :]
        beta = beta_ref[rows, :]
        g_last = jnp.broadcast_to(gc[CHUNK - 1:CHUNK, :], (CHUNK, LANES))
        eg = jnp.exp(gc)
        et = jnp.exp(g_last - gc)
        egl = jnp.exp(g_last)
        gct = jnp.concatenate([gc, gc], axis=0).T

        heads = range(HEADS)
        cols = [slice(h * HEAD_DIM, (h + 1) * HEAD_DIM) for h in heads]
        qh = [qn_ref[rows, cols[h]] for h in heads]
        kh = [kn_ref[rows, cols[h]] for h in heads]
        vh = [vn_ref[rows, cols[h]] for h in heads]
        b_col = [_col(beta, h, lane) for h in heads]
        g_col = [_col(gc, HEADS + h, lane) for h in heads]
        eg_col = [_col(eg, HEADS + h, lane) for h in heads]
        et_col = [_col(et, HEADS + h, lane) for h in heads]
        gl_col = [_col(egl, HEADS + h, lane) for h in heads]
        decay = [jnp.exp(jnp.where(causal, g_col[h] - gct[HEADS + h:HEADS + h + 1, :], -jnp.inf))
                 for h in heads]
        kb = [kh[h] * b_col[h] for h in heads]

        kq = [_dot_nt(jnp.concatenate([kb[h], qh[h]], axis=0).astype(BF16),
                      jnp.concatenate([kh[h], kh[h]], axis=0).astype(BF16)) for h in heads]
        attn = [kq[h][CHUNK:] * decay[h] for h in heads]

        wide = [jnp.where(strict_left, -(kq[h][:CHUNK] * decay[h]), w_init) for h in heads]
        for _ in range(6):
            prod = [_dot(jnp.where(left, wide[h], 0.0).astype(BF16),
                         jnp.concatenate([wide[h], wide[h]], axis=0).astype(BF16)) for h in heads]
            wide = [prod[h] + jnp.where(left, 0.0, wide[h]) for h in heads]

        sol = []
        for h in heads:
            rhs = jnp.concatenate([vh[h] * b_col[h], kb[h] * eg_col[h]], axis=1)
            t_right = jnp.where(left, 0.0, wide[h]).astype(BF16)
            sol.append(_dot(t_right, jnp.concatenate([rhs, rhs], axis=0).astype(BF16)))

        s_old = [s_ref[h] for h in heads]
        ws_qs = [_dot(jnp.concatenate([sol[h][:, HEAD_DIM:], qh[h] * eg_col[h]], axis=0).astype(BF16),
                      s_old[h].astype(BF16)) for h in heads]

        comb = []
        for h in heads:
            v_new = sol[h][:, :HEAD_DIM] - ws_qs[h][:CHUNK]
            vn2 = jnp.concatenate([v_new, v_new], axis=0).astype(BF16)
            kt = kh[h] * et_col[h]
            kt_t = jnp.concatenate([kt, jnp.zeros_like(kt)], axis=0).T
            a_left = jnp.where(left, attn[h], 0.0)
            comb.append(_dot(jnp.concatenate([a_left, kt_t], axis=0).astype(BF16), vn2))

        for h in heads:
            gl2 = jnp.concatenate([gl_col[h], gl_col[h]], axis=0)
            s_ref[h] = s_old[h] * gl2 + comb[h][CHUNK:]
            o = ws_qs[h][CHUNK:] + comb[h][:CHUNK]
            on = o * lax.rsqrt(jnp.mean(o * o, axis=-1, keepdims=True) + EPS) * onw
            zh = z_ref[0, rows, cols[h]]
            o_ref[0, rows, cols[h]] = on * (zh * _sigmoid(zh))
        return carry

    lax.fori_loop(0, n_chunks, chunk_body, 0)


def _gdn_core(proj, conv_w, gate_par, o_norm_w, cs, bsz, seq):
    tb = GDN_BLOCK
    nq = WIDTH // WIDTH
    del nq
    return pl.pallas_call(
        _gdn_kernel,
        grid=(bsz, seq // tb),
        in_specs=[
            pl.BlockSpec((1, tb, WIDTH), lambda b, t: (b, t, 0)),
            pl.BlockSpec((1, tb, WIDTH), lambda b, t: (b, t, 1)),
            pl.BlockSpec((1, tb, WIDTH), lambda b, t: (b, t, 2)),
            pl.BlockSpec((1, tb, WIDTH), lambda b, t: (b, t, 3)),
            pl.BlockSpec((1, tb, LANES), lambda b, t: (b, t, 4 * WIDTH // LANES)),
            pl.BlockSpec((CONV_WIDTH, 3 * WIDTH), lambda b, t: (0, 0)),
            pl.BlockSpec((2, LANES), lambda b, t: (0, 0)),
            pl.BlockSpec((1, HEAD_DIM), lambda b, t: (0, 0)),
            pl.BlockSpec((tb, tb), lambda b, t: (0, 0)),
        ],
        out_specs=pl.BlockSpec((1, tb, WIDTH), lambda b, t: (b, t, 0)),
        out_shape=jax.ShapeDtypeStruct((bsz, seq, WIDTH), F32),
        scratch_shapes=[
            pltpu.VMEM((3, SUBLANES, WIDTH), F32),
            pltpu.VMEM((HEADS, HEAD_DIM, HEAD_DIM), F32),
            pltpu.VMEM((tb, WIDTH), F32),
            pltpu.VMEM((tb, WIDTH), F32),
            pltpu.VMEM((tb, WIDTH), F32),
            pltpu.VMEM((tb, LANES), F32),
            pltpu.VMEM((tb, LANES), F32),
        ],
        compiler_params=pltpu.CompilerParams(
            dimension_semantics=("parallel", "arbitrary"), vmem_limit_bytes=VMEM_LIMIT),
        name="gdn_core",
    )(proj, proj, proj, proj, proj, conv_w, gate_par, o_norm_w.reshape(1, HEAD_DIM), cs)


def _sb_kernel(q_ref, k_ref, v_ref, gate_ref, mo_ref, o_ref, acc_ref, c_ref):
    tq = q_ref.shape[1]
    qi = pl.program_id(2)
    q = q_ref[0]
    mo = mo_ref[...]
    acc_ref[...] = jnp.zeros_like(acc_ref)
    c_ref[...] = jnp.zeros_like(c_ref)
    n_sub = tq // SB_SUB

    def group(kstart, masked):
        kblk = k_ref[0, pl.ds(kstart, tq), :]
        vblk = v_ref[0, pl.ds(kstart, tq), :]
        z = _dot_nt(q, kblk)
        c = c_ref[...]
        probs = [None] * n_sub
        for jj in reversed(range(n_sub)):
            zj = z[:, jj * SB_SUB:(jj + 1) * SB_SUB]
            sp = jnp.maximum(zj, 0.0) + jnp.log(1.0 + jnp.exp2(-jnp.abs(zj))) * LOG2E
            if masked:
                mask = (lax.broadcasted_iota(jnp.int32, (tq, SB_SUB), 0)
                        > lax.broadcasted_iota(jnp.int32, (tq, SB_SUB), 1) + jj * SB_SUB)
                sp = jnp.where(mask, sp, 0.0)
            hi = sp.astype(BF16)
            lo = (sp - hi.astype(F32)).astype(BF16)
            t = _dot(jnp.concatenate([hi, lo], axis=1), mo)
            a = jnp.exp2(zj - sp - (c + t))
            if masked:
                a = jnp.where(mask, a, 0.0)
            probs[jj] = a.astype(BF16)
            c = c + (t[:, :1] + sp[:, :1])
        c_ref[...] = c
        acc_ref[...] += _dot(jnp.concatenate(probs, axis=1), vblk)

    group(pl.multiple_of(qi * tq, tq), True)

    def body(it, carry):
        group(pl.multiple_of((qi - 1 - it) * tq, tq), False)
        return carry

    lax.fori_loop(0, qi, body, 0)
    gate = gate_ref[0]
    o_ref[0] = acc_ref[...] * (gate * _sigmoid(gate))


def _sb_attention(qkv, gate, mo, bsz, seq):
    tq = SB_BLOCK
    return pl.pallas_call(
        _sb_kernel,
        grid=(bsz, HEADS, seq // tq),
        in_specs=[
            pl.BlockSpec((1, tq, HEAD_DIM), lambda b, h, i: (b, i, h)),
            pl.BlockSpec((1, seq, HEAD_DIM), lambda b, h, i: (b, 0, HEADS + h)),
            pl.BlockSpec((1, seq, HEAD_DIM), lambda b, h, i: (b, 0, 2 * HEADS + h)),
            pl.BlockSpec((1, tq, HEAD_DIM), lambda b, h, i: (b, i, h)),
            pl.BlockSpec((2 * SB_SUB, SB_SUB), lambda b, h, i: (0, 0)),
        ],
        out_specs=pl.BlockSpec((1, tq, HEAD_DIM), lambda b, h, i: (b, i, h)),
        out_shape=jax.ShapeDtypeStruct((bsz, seq, WIDTH), F32),
        scratch_shapes=[
            pltpu.VMEM((tq, HEAD_DIM), F32),
            pltpu.VMEM((tq, SB_SUB), F32),
        ],
        compiler_params=pltpu.CompilerParams(
            dimension_semantics=("parallel", "parallel", "arbitrary"), vmem_limit_bytes=VMEM_LIMIT),
        name="sb_attention",
    )(qkv, qkv, qkv, gate, mo)


def _cumsum_matrix(tb):
    i = jnp.arange(tb)
    same_chunk = (i[:, None] // CHUNK) == (i[None, :] // CHUNK)
    return jnp.logical_and(same_chunk, i[:, None] >= i[None, :]).astype(BF16)


def _tail_matrix():
    j = jnp.arange(2 * SB_SUB)[:, None] % SB_SUB
    s = jnp.arange(SB_SUB)[None, :]
    return (j > s).astype(BF16)


def kernel(x, norm_w, a_w_in, a_conv_w, a_a_log, a_dt_bias, a_o_norm, a_w_out, b_w_in, b_w_out, final_norm_w):
    bsz, seq, d = x.shape
    m = bsz * seq
    depth = norm_w.shape[0]
    tm = min(1024, m)
    h2 = x.reshape(m, d)
    for i in range(depth):
        j = i // 2
        last = i == depth - 1
        if i % 2 == 0:
            wa = a_w_in[j]
            pad = jnp.zeros((d, LANES - 2 * HEADS), F32)
            w_all = jnp.concatenate([wa, pad], axis=1).astype(BF16)
            proj = _norm_proj(h2, norm_w[i], w_all, F32, tm, w_all.shape[1] // 3)
            gate_par = jnp.zeros((2, LANES), F32)
            gate_par = gate_par.at[0, HEADS:2 * HEADS].set(a_a_log[j])
            gate_par = gate_par.at[1, HEADS:2 * HEADS].set(a_dt_bias[j])
            og = _gdn_core(proj.reshape(bsz, seq, -1), a_conv_w[j], gate_par, a_o_norm[j],
                           _cumsum_matrix(GDN_BLOCK), bsz, seq)
            w_out = a_w_out[j]
        else:
            wb = b_w_in[j]
            w_qkv = jnp.concatenate([wb[:, :WIDTH] * (HEAD_DIM ** -0.5 * LOG2E), wb[:, WIDTH:3 * WIDTH]], axis=1)
            qkv = _norm_proj(h2, norm_w[i], w_qkv.astype(BF16), BF16, tm, WIDTH)
            gate = _norm_proj(h2, norm_w[i], wb[:, 3 * WIDTH:].astype(BF16), F32, tm, WIDTH)
            og = _sb_attention(qkv.reshape(bsz, seq, -1), gate.reshape(bsz, seq, -1),
                               _tail_matrix(), bsz, seq)
            w_out = b_w_out[j]
        h2 = _out_proj(og.reshape(m, WIDTH), w_out.astype(BF16), h2, final_norm_w, last, min(512, m))
    return h2.reshape(bsz, seq, d)
```

```python
import functools

import jax
import jax.numpy as jnp
from jax import lax
from jax.experimental import pallas as pl
from jax.experimental.pallas import tpu as pltpu

EPS = 1e-6
HEADS = 8
HEAD_DIM = 128
WIDTH = HEADS * HEAD_DIM
CHUNK = 64
CONV_WIDTH = 4
LANES = 128
SUBLANES = 8
GDN_BLOCK = 256
GDN_BATCH = 2
SB_BLOCK = 512
LOG2E = 1.4426950408889634
SB_SUB = 128
VMEM_LIMIT = 48 * 1024 * 1024

F32 = jnp.float32
BF16 = jnp.bfloat16


def _sigmoid(x):
    return 1.0 / (1.0 + jnp.exp2(x * (-LOG2E)))


def _softplus(x):
    return jnp.maximum(x, 0.0) + jnp.log(1.0 + jnp.exp(-jnp.abs(x)))


def _dot(a, b):
    return jnp.dot(a, b, preferred_element_type=F32)


def _dot_nt(a, b):
    return lax.dot_general(a, b, (((1,), (1,)), ((), ())), preferred_element_type=F32)


def _norm_proj_kernel(x_ref, nw_ref, w_ref, o_ref, xn_ref):
    @pl.when(pl.program_id(1) == 0)
    def _():
        x = x_ref[...]
        ms = jnp.mean(x * x, axis=-1, keepdims=True)
        xn_ref[...] = (x * lax.rsqrt(ms + EPS) * nw_ref[...]).astype(BF16)

    o_ref[...] = _dot(xn_ref[...], w_ref[...]).astype(o_ref.dtype)


def _norm_proj(x2, nw, w, out_dtype, tm, tn):
    m, d = x2.shape
    n = w.shape[1]
    return pl.pallas_call(
        _norm_proj_kernel,
        grid=(m // tm, n // tn),
        in_specs=[
            pl.BlockSpec((tm, d), lambda i, j: (i, 0)),
            pl.BlockSpec((1, d), lambda i, j: (0, 0)),
            pl.BlockSpec((d, tn), lambda i, j: (0, j)),
        ],
        out_specs=pl.BlockSpec((tm, tn), lambda i, j: (i, j)),
        out_shape=jax.ShapeDtypeStruct((m, n), out_dtype),
        scratch_shapes=[pltpu.VMEM((tm, d), BF16)],
        compiler_params=pltpu.CompilerParams(
            dimension_semantics=("parallel", "arbitrary"), vmem_limit_bytes=VMEM_LIMIT),
        name="norm_proj",
    )(x2, nw.reshape(1, d), w)


def _out_proj_kernel(a_ref, w_ref, r_ref, fw_ref, o_ref, *, final_norm):
    y = r_ref[...] + _dot(a_ref[...].astype(BF16), w_ref[...])
    if final_norm:
        ms = jnp.mean(y * y, axis=-1, keepdims=True)
        y = y * lax.rsqrt(ms + EPS) * fw_ref[...]
    o_ref[...] = y


def _out_proj(a2, w, res2, fw, final_norm, tm):
    m, d = a2.shape
    n = w.shape[1]
    return pl.pallas_call(
        functools.partial(_out_proj_kernel, final_norm=final_norm),
        grid=(m // tm,),
        in_specs=[
            pl.BlockSpec((tm, d), lambda i: (i, 0)),
            pl.BlockSpec((d, n), lambda i: (0, 0)),
            pl.BlockSpec((tm, n), lambda i: (i, 0)),
            pl.BlockSpec((1, n), lambda i: (0, 0)),
        ],
        out_specs=pl.BlockSpec((tm, n), lambda i: (i, 0)),
        out_shape=jax.ShapeDtypeStruct((m, n), F32),
        compiler_params=pltpu.CompilerParams(
            dimension_semantics=("parallel",), vmem_limit_bytes=VMEM_LIMIT),
        name="out_proj",
    )(a2, w, res2, fw.reshape(1, n))


def _col(x, lane_idx, lane):
    return jnp.sum(jnp.where(lane == lane_idx, x, 0.0), axis=-1, keepdims=True)


def _split3(x):
    h1 = x.astype(BF16)
    r1 = x - h1.astype(F32)
    h2 = r1.astype(BF16)
    h3 = (r1 - h2.astype(F32)).astype(BF16)
    return h1, h2, h3


def _gdn_kernel(q_ref, k_ref, v_ref, z_ref, bg_ref, cw_ref, gp_ref, onw_ref, cs_ref,
                o_ref, tail_ref, s_ref, qn_ref, kn_ref, vn_ref, gc_ref, beta_ref):
    nb, tb = q_ref.shape[0], q_ref.shape[1]
    n_chunks = tb // CHUNK

    @pl.when(pl.program_id(1) == 0)
    def _():
        tail_ref[...] = jnp.zeros_like(tail_ref)
        s_ref[...] = jnp.zeros_like(s_ref)

    cs = cs_ref[...]
    scale = HEAD_DIM ** -0.5
    for bi in range(nb):
        bg = bg_ref[bi]
        beta_ref[bi] = _sigmoid(bg)
        g = -jnp.exp(gp_ref[0:1, :]) * _softplus(bg + gp_ref[1:2, :])
        g1, g2, g3 = _split3(g)
        gc_ref[bi] = _dot(cs, g1) + _dot(cs, g2) + _dot(cs, g3)

        for a, (src, dst) in enumerate(((q_ref, qn_ref), (k_ref, kn_ref), (v_ref, vn_ref))):
            for h in range(HEADS):
                cols = slice(h * HEAD_DIM, (h + 1) * HEAD_DIM)
                wcols = slice(a * WIDTH + h * HEAD_DIM, a * WIDTH + (h + 1) * HEAD_DIM)
                x = src[bi, :, cols]
                xp = jnp.concatenate([tail_ref[bi, a, :, cols], x], axis=0)
                y = x * cw_ref[3:4, wcols]
                for s in range(1, CONV_WIDTH):
                    shifted = pltpu.roll(xp, s, axis=0)[SUBLANES:]
                    y = y + shifted * cw_ref[3 - s:4 - s, wcols]
                tail_ref[bi, a, :, cols] = x[tb - SUBLANES:]
                y = y * _sigmoid(y)
                if a < 2:
                    y = y * lax.rsqrt(jnp.sum(y * y, axis=-1, keepdims=True) + EPS)
                    if a == 0:
                        y = y * scale
                dst[bi, :, cols] = y

    lane = lax.broadcasted_iota(jnp.int32, (CHUNK, LANES), 1)
    row = lax.broadcasted_iota(jnp.int32, (CHUNK, LANES), 0)
    lane_t = lane & (CHUNK - 1)
    left = lane < CHUNK
    causal = row >= lane_t
    w_init = jnp.where(left, 0.0, jnp.where(row == lane_t, 1.0, 0.0))
    strict_left = jnp.logical_and(left, row > lane_t)
    onw = onw_ref[...]

    def chunk_body(c, carry):
        r0 = pl.multiple_of(c * CHUNK, CHUNK)
        rows = pl.ds(r0, CHUNK)
        chains = [(bi, hh) for bi in range(nb) for hh in range(HEADS)]
        heads = range(len(chains))
        cols = [slice(hh * HEAD_DIM, (hh + 1) * HEAD_DIM) for _, hh in chains]
        b_col, g_col, eg_col, et_col, gl_col, g_row = [], [], [], [], [], []
        for bi in range(nb):
            gc = gc_ref[bi, rows, :]
            beta = beta_ref[bi, rows, :]
            g_last = jnp.broadcast_to(gc[CHUNK - 1:CHUNK, :], (CHUNK, LANES))
            eg = jnp.exp(gc)
            et = jnp.exp(g_last - gc)
            egl = jnp.exp(g_last)
            gct = jnp.concatenate([gc, gc], axis=0).T
            for hh in range(HEADS):
                b_col.append(_col(beta, hh, lane))
                g_col.append(_col(gc, HEADS + hh, lane))
                eg_col.append(_col(eg, HEADS + hh, lane))
                et_col.append(_col(et, HEADS + hh, lane))
                gl_col.append(_col(egl, HEADS + hh, lane))
                g_row.append(gct[HEADS + hh:HEADS + hh + 1, :])
        qh = [qn_ref[chains[h][0], rows, cols[h]] for h in heads]
        kh = [kn_ref[chains[h][0], rows, cols[h]] for h in heads]
        vh = [vn_ref[chains[h][0], rows, cols[h]] for h in heads]
        decay = [jnp.exp(jnp.where(causal, g_col[h] - g_row[h], -jnp.inf)) for h in heads]
        kb = [kh[h] * b_col[h] for h in heads]

        kq = [_dot_nt(jnp.concatenate([kb[h], qh[h]], axis=0).astype(BF16),
                      jnp.concatenate([kh[h], kh[h]], axis=0).astype(BF16)) for h in heads]
        attn = [kq[h][CHUNK:] * decay[h] for h in heads]

        wide = [jnp.where(strict_left, -(kq[h][:CHUNK] * decay[h]), w_init) for h in heads]
        for _ in range(6):
            prod = [_dot(jnp.where(left, wide[h], 0.0).astype(BF16),
                         jnp.concatenate([wide[h], wide[h]], axis=0).astype(BF16)) for h in heads]
            wide = [prod[h] + jnp.where(left, 0.0, wide[h]) for h in heads]

        sol = []
        for h in heads:
            rhs = jnp.concatenate([vh[h] * b_col[h], kb[h] * eg_col[h]], axis=1)
            t_right = jnp.where(left, 0.0, wide[h]).astype(BF16)
            sol.append(_dot(t_right, jnp.concatenate([rhs, rhs], axis=0).astype(BF16)))

        s_old = [s_ref[h] for h in heads]
        ws_qs = [_dot(jnp.concatenate([sol[h][:, HEAD_DIM:], qh[h] * eg_col[h]], axis=0).astype(BF16),
                      s_old[h].astype(BF16)) for h in heads]

        comb = []
        for h in heads:
            v_new = sol[h][:, :HEAD_DIM] - ws_qs[h][:CHUNK]
            vn2 = jnp.concatenate([v_new, v_new], axis=0).astype(BF16)
            kt = kh[h] * et_col[h]
            kt_t = jnp.concatenate([kt, jnp.zeros_like(kt)], axis=0).T
            a_left = jnp.where(left, attn[h], 0.0)
            comb.append(_dot(jnp.concatenate([a_left, kt_t], axis=0).astype(BF16), vn2))

        for h in heads:
            gl2 = jnp.concatenate([gl_col[h], gl_col[h]], axis=0)
            s_ref[h] = s_old[h] * gl2 + comb[h][CHUNK:]
            o = ws_qs[h][CHUNK:] + comb[h][:CHUNK]
            on = o * lax.rsqrt(jnp.mean(o * o, axis=-1, keepdims=True) + EPS) * onw
            zh = z_ref[chains[h][0], rows, cols[h]]
            o_ref[chains[h][0], rows, cols[h]] = on * (zh * _sigmoid(zh))
        return carry

    lax.fori_loop(0, n_chunks, chunk_body, 0)


def _gdn_core(proj, conv_w, gate_par, o_norm_w, cs, bsz, seq):
    tb = GDN_BLOCK
    nb = GDN_BATCH if bsz % GDN_BATCH == 0 else 1
    return pl.pallas_call(
        _gdn_kernel,
        grid=(bsz // nb, seq // tb),
        in_specs=[
            pl.BlockSpec((nb, tb, WIDTH), lambda b, t: (b, t, 0)),
            pl.BlockSpec((nb, tb, WIDTH), lambda b, t: (b, t, 1)),
            pl.BlockSpec((nb, tb, WIDTH), lambda b, t: (b, t, 2)),
            pl.BlockSpec((nb, tb, WIDTH), lambda b, t: (b, t, 3)),
            pl.BlockSpec((nb, tb, LANES), lambda b, t: (b, t, 4 * WIDTH // LANES)),
            pl.BlockSpec((CONV_WIDTH, 3 * WIDTH), lambda b, t: (0, 0)),
            pl.BlockSpec((2, LANES), lambda b, t: (0, 0)),
            pl.BlockSpec((1, HEAD_DIM), lambda b, t: (0, 0)),
            pl.BlockSpec((tb, tb), lambda b, t: (0, 0)),
        ],
        out_specs=pl.BlockSpec((nb, tb, WIDTH), lambda b, t: (b, t, 0)),
        out_shape=jax.ShapeDtypeStruct((bsz, seq, WIDTH), F32),
        scratch_shapes=[
            pltpu.VMEM((nb, 3, SUBLANES, WIDTH), F32),
            pltpu.VMEM((nb * HEADS, HEAD_DIM, HEAD_DIM), F32),
            pltpu.VMEM((nb, tb, WIDTH), F32),
            pltpu.VMEM((nb, tb, WIDTH), F32),
            pltpu.VMEM((nb, tb, WIDTH), F32),
            pltpu.VMEM((nb, tb, LANES), F32),
            pltpu.VMEM((nb, tb, LANES), F32),
        ],
        compiler_params=pltpu.CompilerParams(
            dimension_semantics=("parallel", "arbitrary"), vmem_limit_bytes=VMEM_LIMIT),
        name="gdn_core",
    )(proj, proj, proj, proj, proj, conv_w, gate_par, o_norm_w.reshape(1, HEAD_DIM), cs)


def _sb_kernel(q_ref, k_ref, v_ref, gate_ref, mt_ref, o_ref, vt_ref, acc_ref, c_ref, za_ref, pb_ref):
    tq = q_ref.shape[1]
    seq = k_ref.shape[1]
    qi = pl.program_id(2)
    n_sub = tq // SB_SUB

    @pl.when(qi == 0)
    def _():
        def xpose(i, carry):
            r0 = pl.multiple_of(i * tq, tq)
            vt_ref[:, pl.ds(r0, tq)] = v_ref[0, pl.ds(r0, tq), :].astype(F32).T.astype(BF16)
            return carry
        lax.fori_loop(0, seq // tq, xpose, 0)

    q = q_ref[0]
    mt = mt_ref[...]
    acc_ref[...] = jnp.zeros_like(acc_ref)
    c_ref[...] = jnp.zeros_like(c_ref)

    def probabilities(z_sub, c, masked):
        probs = [None] * n_sub
        for jj in reversed(range(n_sub)):
            zj = z_sub(jj)
            sp = jnp.maximum(zj, 0.0) + jnp.log(1.0 + jnp.exp2(-jnp.abs(zj))) * LOG2E
            if masked:
                mask = (lax.broadcasted_iota(jnp.int32, (SB_SUB, tq), 0) + jj * SB_SUB
                        < lax.broadcasted_iota(jnp.int32, (SB_SUB, tq), 1))
                sp = jnp.where(mask, sp, 0.0)
            hi = sp.astype(BF16)
            lo = (sp - hi.astype(F32)).astype(BF16)
            t = _dot(mt, jnp.concatenate([hi, lo], axis=0))
            a = jnp.exp2(zj - sp - t - c)
            if masked:
                a = jnp.where(mask, a, 0.0)
            probs[jj] = a.astype(BF16)
            c = c + (t[0:1, :] + sp[0:1, :])
        return jnp.concatenate(probs, axis=0), c

    def start_of(g):
        return pl.multiple_of(g * tq, tq)

    def scores(g):
        return _dot_nt(k_ref[0, pl.ds(start_of(g), tq), :], q)

    def weighted_values(g, probs):
        return _dot(vt_ref[:, pl.ds(start_of(g), tq)], probs)

    def sub_of(z):
        return lambda jj: z[jj * SB_SUB:(jj + 1) * SB_SUB, :]

    odd = qi % 2
    n_pairs = qi // 2
    first_a = jnp.maximum(qi - 1 - odd, 0)

    def prologue(groups):
        zs = [scores(g) for g, _ in groups]
        z_next = scores(first_a)
        c = c_ref[...]
        out = None
        for (g, masked), z in zip(groups, zs):
            probs, c = probabilities(sub_of(z), c, masked)
            av = weighted_values(g, probs)
            out = av if out is None else out + av
        za_ref[...] = z_next
        pb_ref[...] = jnp.zeros_like(pb_ref)
        c_ref[...] = c
        acc_ref[...] += out

    @pl.when(odd == 0)
    def _():
        prologue([(qi, True)])

    @pl.when(odd == 1)
    def _():
        prologue([(qi, True), (qi - 1, False)])

    def body(it, carry):
        ga = qi - 1 - odd - 2 * it
        gb = ga - 1
        av_prev = weighted_values(ga + 1, pb_ref[...])
        zb = scores(gb)
        probs_a, c = probabilities(lambda jj: za_ref[jj * SB_SUB:(jj + 1) * SB_SUB, :], c_ref[...], False)
        za_ref[...] = scores(jnp.maximum(ga - 2, 0))
        av_a = weighted_values(ga, probs_a)
        probs_b, c = probabilities(sub_of(zb), c, False)
        pb_ref[...] = probs_b
        c_ref[...] = c
        acc_ref[...] += av_prev + av_a
        return carry

    lax.fori_loop(0, n_pairs, body, 0)
    acc_ref[...] += weighted_values(0, pb_ref[...])
    gate = gate_ref[0]
    o_ref[0] = acc_ref[...].T * (gate * _sigmoid(gate))


def _sb_attention(qkv, gate, mo, bsz, seq):
    tq = SB_BLOCK
    return pl.pallas_call(
        _sb_kernel,
        grid=(bsz, HEADS, seq // tq),
        in_specs=[
            pl.BlockSpec((1, tq, HEAD_DIM), lambda b, h, i: (b, i, h)),
            pl.BlockSpec((1, seq, HEAD_DIM), lambda b, h, i: (b, 0, HEADS + h)),
            pl.BlockSpec((1, seq, HEAD_DIM), lambda b, h, i: (b, 0, 2 * HEADS + h)),
            pl.BlockSpec((1, tq, HEAD_DIM), lambda b, h, i: (b, i, h)),
            pl.BlockSpec((SB_SUB, 2 * SB_SUB), lambda b, h, i: (0, 0)),
        ],
        out_specs=pl.BlockSpec((1, tq, HEAD_DIM), lambda b, h, i: (b, i, h)),
        out_shape=jax.ShapeDtypeStruct((bsz, seq, WIDTH), F32),
        scratch_shapes=[
            pltpu.VMEM((HEAD_DIM, seq), BF16),
            pltpu.VMEM((HEAD_DIM, tq), F32),
            pltpu.VMEM((1, tq), F32),
            pltpu.VMEM((tq, tq), F32),
            pltpu.VMEM((tq, tq), BF16),
        ],
        compiler_params=pltpu.CompilerParams(
            dimension_semantics=("parallel", "parallel", "arbitrary"), vmem_limit_bytes=VMEM_LIMIT),
        name="sb_attention",
    )(qkv, qkv, qkv, gate, mo)


def _cumsum_matrix(tb):
    i = jnp.arange(tb)
    same_chunk = (i[:, None] // CHUNK) == (i[None, :] // CHUNK)
    return jnp.logical_and(same_chunk, i[:, None] >= i[None, :]).astype(BF16)


def _tail_matrix():
    s = jnp.arange(SB_SUB)[:, None]
    j = jnp.arange(2 * SB_SUB)[None, :] % SB_SUB
    return (j > s).astype(BF16)


def kernel(x, norm_w, a_w_in, a_conv_w, a_a_log, a_dt_bias, a_o_norm, a_w_out, b_w_in, b_w_out, final_norm_w):
    bsz, seq, d = x.shape
    m = bsz * seq
    depth = norm_w.shape[0]
    tm = min(1024, m)
    h2 = x.reshape(m, d)
    for i in range(depth):
        j = i // 2
        last = i == depth - 1
        if i % 2 == 0:
            wa = a_w_in[j]
            pad = jnp.zeros((d, LANES - 2 * HEADS), F32)
            w_all = jnp.concatenate([wa, pad], axis=1).astype(BF16)
            proj = _norm_proj(h2, norm_w[i], w_all, F32, tm, w_all.shape[1] // 3)
            gate_par = jnp.zeros((2, LANES), F32)
            gate_par = gate_par.at[0, HEADS:2 * HEADS].set(a_a_log[j])
            gate_par = gate_par.at[1, HEADS:2 * HEADS].set(a_dt_bias[j])
            og = _gdn_core(proj.reshape(bsz, seq, -1), a_conv_w[j], gate_par, a_o_norm[j],
                           _cumsum_matrix(GDN_BLOCK), bsz, seq)
            w_out = a_w_out[j]
        else:
            wb = b_w_in[j]
            w_qkv = jnp.concatenate([wb[:, :WIDTH] * (HEAD_DIM ** -0.5 * LOG2E), wb[:, WIDTH:3 * WIDTH]], axis=1)
            qkv = _norm_proj(h2, norm_w[i], w_qkv.astype(BF16), BF16, tm, WIDTH)
            gate = _norm_proj(h2, norm_w[i], wb[:, 3 * WIDTH:].astype(BF16), F32, tm, WIDTH)
            og = _sb_attention(qkv.reshape(bsz, seq, -1), gate.reshape(bsz, seq, -1),
                               _tail_matrix(), bsz, seq)
            w_out = b_w_out[j]
        h2 = _out_proj(og.reshape(m, WIDTH), w_out.astype(BF16), h2, final_norm_w, last, min(512, m))
    return h2.reshape(bsz, seq, d)
```

```python
import functools

import jax
import jax.numpy as jnp
from jax import lax
from jax.experimental import pallas as pl
from jax.experimental.pallas import tpu as pltpu

EPS = 1e-6
HEADS = 8
HEAD_DIM = 128
WIDTH = HEADS * HEAD_DIM
CHUNK = 64
CONV_WIDTH = 4
LANES = 128
SUBLANES = 8
GDN_BLOCK = 256
GDN_BATCH = 2
SB_BLOCK = 512
LOG2E = 1.4426950408889634
SB_SUB = 128
VMEM_LIMIT = 48 * 1024 * 1024

F32 = jnp.float32
BF16 = jnp.bfloat16


def _sigmoid(x):
    return 1.0 / (1.0 + jnp.exp2(x * (-LOG2E)))


def _softplus(x):
    return jnp.maximum(x, 0.0) + jnp.log(1.0 + jnp.exp(-jnp.abs(x)))


def _dot(a, b):
    return jnp.dot(a, b, preferred_element_type=F32)


def _dot_nt(a, b):
    return lax.dot_general(a, b, (((1,), (1,)), ((), ())), preferred_element_type=F32)


def _norm_proj_kernel(x_ref, nw_ref, w_ref, o_ref, xn_ref):
    @pl.when(pl.program_id(1) == 0)
    def _():
        x = x_ref[...]
        ms = jnp.mean(x * x, axis=-1, keepdims=True)
        xn_ref[...] = (x * lax.rsqrt(ms + EPS) * nw_ref[...]).astype(BF16)

    o_ref[...] = _dot(xn_ref[...], w_ref[...]).astype(o_ref.dtype)


def _norm_proj(x2, nw, w, out_dtype, tm, tn):
    m, d = x2.shape
    n = w.shape[1]
    return pl.pallas_call(
        _norm_proj_kernel,
        grid=(m // tm, n // tn),
        in_specs=[
            pl.BlockSpec((tm, d), lambda i, j: (i, 0)),
            pl.BlockSpec((1, d), lambda i, j: (0, 0)),
            pl.BlockSpec((d, tn), lambda i, j: (0, j)),
        ],
        out_specs=pl.BlockSpec((tm, tn), lambda i, j: (i, j)),
        out_shape=jax.ShapeDtypeStruct((m, n), out_dtype),
        scratch_shapes=[pltpu.VMEM((tm, d), BF16)],
        compiler_params=pltpu.CompilerParams(
            dimension_semantics=("parallel", "arbitrary"), vmem_limit_bytes=VMEM_LIMIT),
        name="norm_proj",
    )(x2, nw.reshape(1, d), w)


def _out_proj_kernel(a_ref, w_ref, r_ref, fw_ref, o_ref, *, final_norm):
    y = r_ref[...] + _dot(a_ref[...].astype(BF16), w_ref[...])
    if final_norm:
        ms = jnp.mean(y * y, axis=-1, keepdims=True)
        y = y * lax.rsqrt(ms + EPS) * fw_ref[...]
    o_ref[...] = y


def _out_proj(a2, w, res2, fw, final_norm, tm):
    m, d = a2.shape
    n = w.shape[1]
    return pl.pallas_call(
        functools.partial(_out_proj_kernel, final_norm=final_norm),
        grid=(m // tm,),
        in_specs=[
            pl.BlockSpec((tm, d), lambda i: (i, 0)),
            pl.BlockSpec((d, n), lambda i: (0, 0)),
            pl.BlockSpec((tm, n), lambda i: (i, 0)),
            pl.BlockSpec((1, n), lambda i: (0, 0)),
        ],
        out_specs=pl.BlockSpec((tm, n), lambda i: (i, 0)),
        out_shape=jax.ShapeDtypeStruct((m, n), F32),
        compiler_params=pltpu.CompilerParams(
            dimension_semantics=("parallel",), vmem_limit_bytes=VMEM_LIMIT),
        name="out_proj",
    )(a2, w, res2, fw.reshape(1, n))


def _col(x, lane_idx, lane):
    return jnp.sum(jnp.where(lane == lane_idx, x, 0.0), axis=-1, keepdims=True)


def _split3(x):
    h1 = x.astype(BF16)
    r1 = x - h1.astype(F32)
    h2 = r1.astype(BF16)
    h3 = (r1 - h2.astype(F32)).astype(BF16)
    return h1, h2, h3


def _gdn_kernel(q_ref, k_ref, v_ref, z_ref, bg_ref, cw_ref, gp_ref, onw_ref, cs_ref,
                o_ref, tail_ref, s_ref, qn_ref, kn_ref, vn_ref, gc_ref, beta_ref):
    nb, tb = q_ref.shape[0], q_ref.shape[1]
    n_chunks = tb // CHUNK

    @pl.when(pl.program_id(1) == 0)
    def _():
        tail_ref[...] = jnp.zeros_like(tail_ref)
        s_ref[...] = jnp.zeros_like(s_ref)

    cs = cs_ref[...]
    scale = HEAD_DIM ** -0.5
    for bi in range(nb):
        bg = bg_ref[bi]
        beta_ref[bi] = _sigmoid(bg)
        g = -jnp.exp(gp_ref[0:1, :]) * _softplus(bg + gp_ref[1:2, :])
        g1, g2, g3 = _split3(g)
        gc_ref[bi] = _dot(cs, g1) + _dot(cs, g2) + _dot(cs, g3)

        for a, (src, dst) in enumerate(((q_ref, qn_ref), (k_ref, kn_ref), (v_ref, vn_ref))):
            for h in range(HEADS):
                cols = slice(h * HEAD_DIM, (h + 1) * HEAD_DIM)
                wcols = slice(a * WIDTH + h * HEAD_DIM, a * WIDTH + (h + 1) * HEAD_DIM)
                x = src[bi, :, cols]
                xp = jnp.concatenate([tail_ref[bi, a, :, cols], x], axis=0)
                y = x * cw_ref[3:4, wcols]
                for s in range(1, CONV_WIDTH):
                    shifted = pltpu.roll(xp, s, axis=0)[SUBLANES:]
                    y = y + shifted * cw_ref[3 - s:4 - s, wcols]
                tail_ref[bi, a, :, cols] = x[tb - SUBLANES:]
                y = y * _sigmoid(y)
                if a < 2:
                    y = y * lax.rsqrt(jnp.sum(y * y, axis=-1, keepdims=True) + EPS)
                    if a == 0:
                        y = y * scale
                dst[bi, :, cols] = y

    lane = lax.broadcasted_iota(jnp.int32, (CHUNK, LANES), 1)
    row = lax.broadcasted_iota(jnp.int32, (CHUNK, LANES), 0)
    lane_t = lane & (CHUNK - 1)
    left = lane < CHUNK
    causal = row >= lane_t
    w_init = jnp.where(left, 0.0, jnp.where(row == lane_t, 1.0, 0.0))
    strict_left = jnp.logical_and(left, row > lane_t)
    onw = onw_ref[...]

    def chunk_body(c, carry):
        r0 = pl.multiple_of(c * CHUNK, CHUNK)
        rows = pl.ds(r0, CHUNK)
        chains = [(bi, hh) for bi in range(nb) for hh in range(HEADS)]
        heads = range(len(chains))
        cols = [slice(hh * HEAD_DIM, (hh + 1) * HEAD_DIM) for _, hh in chains]
        b_col, g_col, eg_col, et_col, gl_col, g_row = [], [], [], [], [], []
        for bi in range(nb):
            gc = gc_ref[bi, rows, :]
            beta = beta_ref[bi, rows, :]
            g_last = jnp.broadcast_to(gc[CHUNK - 1:CHUNK, :], (CHUNK, LANES))
            eg = jnp.exp(gc)
            et = jnp.exp(g_last - gc)
            egl = jnp.exp(g_last)
            gct = jnp.concatenate([gc, gc], axis=0).T
            for hh in range(HEADS):
                b_col.append(_col(beta, hh, lane))
                g_col.append(_col(gc, HEADS + hh, lane))
                eg_col.append(_col(eg, HEADS + hh, lane))
                et_col.append(_col(et, HEADS + hh, lane))
                gl_col.append(_col(egl, HEADS + hh, lane))
                g_row.append(gct[HEADS + hh:HEADS + hh + 1, :])
        qh = [qn_ref[chains[h][0], rows, cols[h]] for h in heads]
        kh = [kn_ref[chains[h][0], rows, cols[h]] for h in heads]
        vh = [vn_ref[chains[h][0], rows, cols[h]] for h in heads]
        decay = [jnp.exp(jnp.where(causal, g_col[h] - g_row[h], -jnp.inf)) for h in heads]
        kb = [kh[h] * b_col[h] for h in heads]

        kq = [_dot_nt(jnp.concatenate([kb[h], qh[h]], axis=0).astype(BF16),
                      jnp.concatenate([kh[h], kh[h]], axis=0).astype(BF16)) for h in heads]
        attn = [kq[h][CHUNK:] * decay[h] for h in heads]

        wide = [jnp.where(strict_left, -(kq[h][:CHUNK] * decay[h]), w_init) for h in heads]
        for _ in range(6):
            prod = [_dot(jnp.where(left, wide[h], 0.0).astype(BF16),
                         jnp.concatenate([wide[h], wide[h]], axis=0).astype(BF16)) for h in heads]
            wide = [prod[h] + jnp.where(left, 0.0, wide[h]) for h in heads]

        sol = []
        for h in heads:
            rhs = jnp.concatenate([vh[h] * b_col[h], kb[h] * eg_col[h]], axis=1)
            t_right = jnp.where(left, 0.0, wide[h]).astype(BF16)
            sol.append(_dot(t_right, jnp.concatenate([rhs, rhs], axis=0).astype(BF16)))

        s_old = [s_ref[h] for h in heads]
        ws_qs = [_dot(jnp.concatenate([sol[h][:, HEAD_DIM:], qh[h] * eg_col[h]], axis=0).astype(BF16),
                      s_old[h].astype(BF16)) for h in heads]

        comb = []
        for h in heads:
            v_new = sol[h][:, :HEAD_DIM] - ws_qs[h][:CHUNK]
            vn2 = jnp.concatenate([v_new, v_new], axis=0).astype(BF16)
            kt = kh[h] * et_col[h]
            kt_t = jnp.concatenate([kt, jnp.zeros_like(kt)], axis=0).T
            a_left = jnp.where(left, attn[h], 0.0)
            comb.append(_dot(jnp.concatenate([a_left, kt_t], axis=0).astype(BF16), vn2))

        for h in heads:
            gl2 = jnp.concatenate([gl_col[h], gl_col[h]], axis=0)
            s_ref[h] = s_old[h] * gl2 + comb[h][CHUNK:]
            o = ws_qs[h][CHUNK:] + comb[h][:CHUNK]
            on = o * lax.rsqrt(jnp.mean(o * o, axis=-1, keepdims=True) + EPS) * onw
            zh = z_ref[chains[h][0], rows, cols[h]]
            o_ref[chains[h][0], rows, cols[h]] = on * (zh * _sigmoid(zh))
        return carry

    lax.fori_loop(0, n_chunks, chunk_body, 0)


def _gdn_core(proj, conv_w, gate_par, o_norm_w, cs, bsz, seq):
    tb = GDN_BLOCK
    nb = GDN_BATCH if bsz % GDN_BATCH == 0 else 1
    return pl.pallas_call(
        _gdn_kernel,
        grid=(bsz // nb, seq // tb),
        in_specs=[
            pl.BlockSpec((nb, tb, WIDTH), lambda b, t: (b, t, 0)),
            pl.BlockSpec((nb, tb, WIDTH), lambda b, t: (b, t, 1)),
            pl.BlockSpec((nb, tb, WIDTH), lambda b, t: (b, t, 2)),
            pl.BlockSpec((nb, tb, WIDTH), lambda b, t: (b, t, 3)),
            pl.BlockSpec((nb, tb, LANES), lambda b, t: (b, t, 4 * WIDTH // LANES)),
            pl.BlockSpec((CONV_WIDTH, 3 * WIDTH), lambda b, t: (0, 0)),
            pl.BlockSpec((2, LANES), lambda b, t: (0, 0)),
            pl.BlockSpec((1, HEAD_DIM), lambda b, t: (0, 0)),
            pl.BlockSpec((tb, tb), lambda b, t: (0, 0)),
        ],
        out_specs=pl.BlockSpec((nb, tb, WIDTH), lambda b, t: (b, t, 0)),
        out_shape=jax.ShapeDtypeStruct((bsz, seq, WIDTH), F32),
        scratch_shapes=[
            pltpu.VMEM((nb, 3, SUBLANES, WIDTH), F32),
            pltpu.VMEM((nb * HEADS, HEAD_DIM, HEAD_DIM), F32),
            pltpu.VMEM((nb, tb, WIDTH), F32),
            pltpu.VMEM((nb, tb, WIDTH), F32),
            pltpu.VMEM((nb, tb, WIDTH), F32),
            pltpu.VMEM((nb, tb, LANES), F32),
            pltpu.VMEM((nb, tb, LANES), F32),
        ],
        compiler_params=pltpu.CompilerParams(
            dimension_semantics=("parallel", "arbitrary"), vmem_limit_bytes=VMEM_LIMIT),
        name="gdn_core",
    )(proj, proj, proj, proj, proj, conv_w, gate_par, o_norm_w.reshape(1, HEAD_DIM), cs)


def _sb_kernel(q_ref, k_ref, v_ref, gate_ref, mt_ref, o_ref, vt_ref, acc_ref, c_ref, za_ref, pb_ref):
    tq = q_ref.shape[1]
    seq = k_ref.shape[1]
    qi = pl.program_id(2)
    n_sub = tq // SB_SUB

    @pl.when(qi == 0)
    def _():
        def xpose(i, carry):
            r0 = pl.multiple_of(i * tq, tq)
            vt_ref[:, pl.ds(r0, tq)] = v_ref[0, pl.ds(r0, tq), :].astype(F32).T.astype(BF16)
            return carry
        lax.fori_loop(0, seq // tq, xpose, 0)

    q = q_ref[0]
    mt = mt_ref[...]
    acc_ref[...] = jnp.zeros_like(acc_ref)
    c_ref[...] = jnp.zeros_like(c_ref)

    def probabilities(z_sub, c, masked):
        probs = [None] * n_sub
        for jj in reversed(range(n_sub)):
            q0 = jj * SB_SUB if masked else 0
            zj = z_sub(jj)[:, q0:]
            cj = c[:, q0:]
            sp = jnp.maximum(zj, 0.0) + jnp.log(1.0 + jnp.exp2(-jnp.abs(zj))) * LOG2E
            if masked:
                mask = (lax.broadcasted_iota(jnp.int32, zj.shape, 0)
                        < lax.broadcasted_iota(jnp.int32, zj.shape, 1))
                sp = jnp.where(mask, sp, 0.0)
            t = _dot(mt, sp.astype(BF16))
            a = jnp.exp2(zj - sp - t - cj)
            if masked:
                a = jnp.where(mask, a, 0.0)
            a = a.astype(BF16)
            cj = cj + (t[0:1, :] + sp[0:1, :])
            if q0:
                a = jnp.concatenate([jnp.zeros((SB_SUB, q0), BF16), a], axis=1)
                cj = jnp.concatenate([c[:, :q0], cj], axis=1)
            probs[jj] = a
            c = cj
        return jnp.concatenate(probs, axis=0), c

    def start_of(g):
        return pl.multiple_of(g * tq, tq)

    def scores(g):
        return _dot_nt(k_ref[0, pl.ds(start_of(g), tq), :], q)

    def weighted_values(g, probs):
        return _dot(vt_ref[:, pl.ds(start_of(g), tq)], probs)

    def sub_of(z):
        return lambda jj: z[jj * SB_SUB:(jj + 1) * SB_SUB, :]

    odd = qi % 2
    n_pairs = qi // 2
    first_a = jnp.maximum(qi - 1 - odd, 0)

    def prologue(groups):
        zs = [scores(g) for g, _ in groups]
        z_next = scores(first_a)
        c = c_ref[...]
        for i, ((g, masked), z) in enumerate(zip(groups, zs)):
            probs, c = probabilities(sub_of(z), c, masked)
            if i + 1 < len(groups):
                acc_ref[...] += weighted_values(g, probs)
            else:
                pb_ref[...] = probs
        za_ref[...] = z_next
        c_ref[...] = c

    @pl.when(odd == 0)
    def _():
        prologue([(qi, True)])

    @pl.when(odd == 1)
    def _():
        prologue([(qi, True), (qi - 1, False)])

    def body(it, carry):
        ga = qi - 1 - odd - 2 * it
        gb = ga - 1
        av_prev = weighted_values(ga + 1, pb_ref[...])
        zb = scores(gb)
        probs_a, c = probabilities(lambda jj: za_ref[jj * SB_SUB:(jj + 1) * SB_SUB, :], c_ref[...], False)
        za_ref[...] = scores(jnp.maximum(ga - 2, 0))
        av_a = weighted_values(ga, probs_a)
        probs_b, c = probabilities(sub_of(zb), c, False)
        pb_ref[...] = probs_b
        c_ref[...] = c
        acc_ref[...] += av_prev + av_a
        return carry

    lax.fori_loop(0, n_pairs, body, 0)
    acc_ref[...] += weighted_values(0, pb_ref[...])
    gate = gate_ref[0]
    o_ref[0] = acc_ref[...].T * (gate * _sigmoid(gate))


def _sb_attention(qkv, gate, mo, bsz, seq):
    tq = SB_BLOCK
    return pl.pallas_call(
        _sb_kernel,
        grid=(bsz, HEADS, seq // tq),
        in_specs=[
            pl.BlockSpec((1, tq, HEAD_DIM), lambda b, h, i: (b, i, h)),
            pl.BlockSpec((1, seq, HEAD_DIM), lambda b, h, i: (b, 0, HEADS + h)),
            pl.BlockSpec((1, seq, HEAD_DIM), lambda b, h, i: (b, 0, 2 * HEADS + h)),
            pl.BlockSpec((1, tq, HEAD_DIM), lambda b, h, i: (b, i, h)),
            pl.BlockSpec((SB_SUB, SB_SUB), lambda b, h, i: (0, 0)),
        ],
        out_specs=pl.BlockSpec((1, tq, HEAD_DIM), lambda b, h, i: (b, i, h)),
        out_shape=jax.ShapeDtypeStruct((bsz, seq, WIDTH), F32),
        scratch_shapes=[
            pltpu.VMEM((HEAD_DIM, seq), BF16),
            pltpu.VMEM((HEAD_DIM, tq), F32),
            pltpu.VMEM((1, tq), F32),
            pltpu.VMEM((tq, tq), F32),
            pltpu.VMEM((tq, tq), BF16),
        ],
        compiler_params=pltpu.CompilerParams(
            dimension_semantics=("parallel", "parallel", "arbitrary"), vmem_limit_bytes=VMEM_LIMIT),
        name="sb_attention",
    )(qkv, qkv, qkv, gate, mo)


def _cumsum_matrix(tb):
    i = jnp.arange(tb)
    same_chunk = (i[:, None] // CHUNK) == (i[None, :] // CHUNK)
    return jnp.logical_and(same_chunk, i[:, None] >= i[None, :]).astype(BF16)


def _tail_matrix():
    s = jnp.arange(SB_SUB)[:, None]
    j = jnp.arange(SB_SUB)[None, :]
    return (j > s).astype(BF16)


def kernel(x, norm_w, a_w_in, a_conv_w, a_a_log, a_dt_bias, a_o_norm, a_w_out, b_w_in, b_w_out, final_norm_w):
    bsz, seq, d = x.shape
    m = bsz * seq
    depth = norm_w.shape[0]
    tm = min(1024, m)
    h2 = x.reshape(m, d)
    for i in range(depth):
        j = i // 2
        last = i == depth - 1
        if i % 2 == 0:
            wa = a_w_in[j]
            pad = jnp.zeros((d, LANES - 2 * HEADS), F32)
            w_all = jnp.concatenate([wa, pad], axis=1).astype(BF16)
            proj = _norm_proj(h2, norm_w[i], w_all, F32, tm, w_all.shape[1] // 3)
            gate_par = jnp.zeros((2, LANES), F32)
            gate_par = gate_par.at[0, HEADS:2 * HEADS].set(a_a_log[j])
            gate_par = gate_par.at[1, HEADS:2 * HEADS].set(a_dt_bias[j])
            og = _gdn_core(proj.reshape(bsz, seq, -1), a_conv_w[j], gate_par, a_o_norm[j],
                           _cumsum_matrix(GDN_BLOCK), bsz, seq)
            w_out = a_w_out[j]
        else:
            wb = b_w_in[j]
            w_qkv = jnp.concatenate([wb[:, :WIDTH] * (HEAD_DIM ** -0.5 * LOG2E), wb[:, WIDTH:3 * WIDTH]], axis=1)
            qkv = _norm_proj(h2, norm_w[i], w_qkv.astype(BF16), BF16, tm, WIDTH)
            gate = _norm_proj(h2, norm_w[i], wb[:, 3 * WIDTH:].astype(BF16), F32, tm, WIDTH)
            og = _sb_attention(qkv.reshape(bsz, seq, -1), gate.reshape(bsz, seq, -1),
                               _tail_matrix(), bsz, seq)
            w_out = b_w_out[j]
        h2 = _out_proj(og.reshape(m, WIDTH), w_out.astype(BF16), h2, final_norm_w, last, min(512, m))
    return h2.reshape(bsz, seq, d)
```

```python
import functools

import jax
import jax.numpy as jnp
from jax import lax
from jax.experimental import pallas as pl
from jax.experimental.pallas import tpu as pltpu

EPS = 1e-6
HEADS = 8
HEAD_DIM = 128
WIDTH = HEADS * HEAD_DIM
CHUNK = 64
CONV_WIDTH = 4
LANES = 128
SUBLANES = 8
GDN_BLOCK = 256
GDN_BATCH = 2
SB_BLOCK = 512
LOG2E = 1.4426950408889634
SB_SUB = 128
VMEM_LIMIT = 48 * 1024 * 1024

F32 = jnp.float32
BF16 = jnp.bfloat16


def _sigmoid(x):
    return 1.0 / (1.0 + jnp.exp2(x * (-LOG2E)))


def _softplus(x):
    return jnp.maximum(x, 0.0) + jnp.log(1.0 + jnp.exp(-jnp.abs(x)))


def _dot(a, b):
    return jnp.dot(a, b, preferred_element_type=F32)


def _dot_nt(a, b):
    return lax.dot_general(a, b, (((1,), (1,)), ((), ())), preferred_element_type=F32)


def _norm_proj_kernel(x_ref, nw_ref, w_ref, o_ref, xn_ref):
    @pl.when(pl.program_id(1) == 0)
    def _():
        x = x_ref[...]
        ms = jnp.mean(x * x, axis=-1, keepdims=True)
        xn_ref[...] = (x * lax.rsqrt(ms + EPS) * nw_ref[...]).astype(BF16)

    o_ref[...] = _dot(xn_ref[...], w_ref[...]).astype(o_ref.dtype)


def _norm_proj(x2, nw, w, out_dtype, tm, tn):
    m, d = x2.shape
    n = w.shape[1]
    return pl.pallas_call(
        _norm_proj_kernel,
        grid=(m // tm, n // tn),
        in_specs=[
            pl.BlockSpec((tm, d), lambda i, j: (i, 0)),
            pl.BlockSpec((1, d), lambda i, j: (0, 0)),
            pl.BlockSpec((d, tn), lambda i, j: (0, j)),
        ],
        out_specs=pl.BlockSpec((tm, tn), lambda i, j: (i, j)),
        out_shape=jax.ShapeDtypeStruct((m, n), out_dtype),
        scratch_shapes=[pltpu.VMEM((tm, d), BF16)],
        compiler_params=pltpu.CompilerParams(
            dimension_semantics=("parallel", "arbitrary"), vmem_limit_bytes=VMEM_LIMIT),
        name="norm_proj",
    )(x2, nw.reshape(1, d), w)


def _out_proj_kernel(a_ref, w_ref, r_ref, fw_ref, o_ref, *, final_norm):
    y = r_ref[...] + _dot(a_ref[...].astype(BF16), w_ref[...])
    if final_norm:
        ms = jnp.mean(y * y, axis=-1, keepdims=True)
        y = y * lax.rsqrt(ms + EPS) * fw_ref[...]
    o_ref[...] = y


def _out_proj(a2, w, res2, fw, final_norm, tm):
    m, d = a2.shape
    n = w.shape[1]
    return pl.pallas_call(
        functools.partial(_out_proj_kernel, final_norm=final_norm),
        grid=(m // tm,),
        in_specs=[
            pl.BlockSpec((tm, d), lambda i: (i, 0)),
            pl.BlockSpec((d, n), lambda i: (0, 0)),
            pl.BlockSpec((tm, n), lambda i: (i, 0)),
            pl.BlockSpec((1, n), lambda i: (0, 0)),
        ],
        out_specs=pl.BlockSpec((tm, n), lambda i: (i, 0)),
        out_shape=jax.ShapeDtypeStruct((m, n), F32),
        compiler_params=pltpu.CompilerParams(
            dimension_semantics=("parallel",), vmem_limit_bytes=VMEM_LIMIT),
        name="out_proj",
    )(a2, w, res2, fw.reshape(1, n))


def _col(x, lane_idx, lane):
    return jnp.sum(jnp.where(lane == lane_idx, x, 0.0), axis=-1, keepdims=True)


def _split3(x):
    h1 = x.astype(BF16)
    r1 = x - h1.astype(F32)
    h2 = r1.astype(BF16)
    h3 = (r1 - h2.astype(F32)).astype(BF16)
    return h1, h2, h3


def _gdn_kernel(q_ref, k_ref, v_ref, z_ref, bg_ref, cw_ref, gp_ref, onw_ref, cs_ref,
                o_ref, tail_ref, s_ref, qn_ref, kn_ref, vn_ref, gc_ref, beta_ref):
    nb, tb = q_ref.shape[0], q_ref.shape[1]
    n_chunks = tb // CHUNK

    @pl.when(pl.program_id(1) == 0)
    def _():
        tail_ref[...] = jnp.zeros_like(tail_ref)
        s_ref[...] = jnp.zeros_like(s_ref)

    cs = cs_ref[...]
    scale = HEAD_DIM ** -0.5
    for bi in range(nb):
        bg = bg_ref[bi]
        beta_ref[bi] = _sigmoid(bg)
        g = -jnp.exp(gp_ref[0:1, :]) * _softplus(bg + gp_ref[1:2, :])
        g1, g2, g3 = _split3(g)
        gc_ref[bi] = _dot(cs, g1) + _dot(cs, g2) + _dot(cs, g3)

        for a, (src, dst) in enumerate(((q_ref, qn_ref), (k_ref, kn_ref), (v_ref, vn_ref))):
            for h in range(HEADS):
                cols = slice(h * HEAD_DIM, (h + 1) * HEAD_DIM)
                wcols = slice(a * WIDTH + h * HEAD_DIM, a * WIDTH + (h + 1) * HEAD_DIM)
                x = src[bi, :, cols]
                xp = jnp.concatenate([tail_ref[bi, a, :, cols], x], axis=0)
                y = x * cw_ref[3:4, wcols]
                for s in range(1, CONV_WIDTH):
                    shifted = pltpu.roll(xp, s, axis=0)[SUBLANES:]
                    y = y + shifted * cw_ref[3 - s:4 - s, wcols]
                tail_ref[bi, a, :, cols] = x[tb - SUBLANES:]
                y = y * _sigmoid(y)
                if a < 2:
                    y = y * lax.rsqrt(jnp.sum(y * y, axis=-1, keepdims=True) + EPS)
                    if a == 0:
                        y = y * scale
                dst[bi, :, cols] = y

    lane = lax.broadcasted_iota(jnp.int32, (CHUNK, LANES), 1)
    row = lax.broadcasted_iota(jnp.int32, (CHUNK, LANES), 0)
    lane_t = lane & (CHUNK - 1)
    left = lane < CHUNK
    causal = row >= lane_t
    w_init = jnp.where(left, 0.0, jnp.where(row == lane_t, 1.0, 0.0))
    strict_left = jnp.logical_and(left, row > lane_t)
    onw = onw_ref[...]

    def chunk_body(c, carry):
        r0 = pl.multiple_of(c * CHUNK, CHUNK)
        rows = pl.ds(r0, CHUNK)
        chains = [(bi, hh) for bi in range(nb) for hh in range(HEADS)]
        heads = range(len(chains))
        cols = [slice(hh * HEAD_DIM, (hh + 1) * HEAD_DIM) for _, hh in chains]
        b_col, g_col, eg_col, et_col, gl_col, g_row = [], [], [], [], [], []
        for bi in range(nb):
            gc = gc_ref[bi, rows, :]
            beta = beta_ref[bi, rows, :]
            g_last = jnp.broadcast_to(gc[CHUNK - 1:CHUNK, :], (CHUNK, LANES))
            eg = jnp.exp(gc)
            et = jnp.exp(g_last - gc)
            egl = jnp.exp(g_last)
            gct = jnp.concatenate([gc, gc], axis=0).T
            for hh in range(HEADS):
                b_col.append(_col(beta, hh, lane))
                g_col.append(_col(gc, HEADS + hh, lane))
                eg_col.append(_col(eg, HEADS + hh, lane))
                et_col.append(_col(et, HEADS + hh, lane))
                gl_col.append(_col(egl, HEADS + hh, lane))
                g_row.append(gct[HEADS + hh:HEADS + hh + 1, :])
        qh = [qn_ref[chains[h][0], rows, cols[h]] for h in heads]
        kh = [kn_ref[chains[h][0], rows, cols[h]] for h in heads]
        vh = [vn_ref[chains[h][0], rows, cols[h]] for h in heads]
        decay = [jnp.exp(jnp.where(causal, g_col[h] - g_row[h], -jnp.inf)) for h in heads]
        kb = [kh[h] * b_col[h] for h in heads]

        kq = [_dot_nt(jnp.concatenate([kb[h], qh[h]], axis=0).astype(BF16),
                      jnp.concatenate([kh[h], kh[h]], axis=0).astype(BF16)) for h in heads]
        attn = [kq[h][CHUNK:] * decay[h] for h in heads]

        wide = [jnp.where(strict_left, -(kq[h][:CHUNK] * decay[h]), w_init) for h in heads]
        zero_rows = jnp.zeros((CHUNK, LANES), BF16)
        for _ in range(6):
            wb = [wide[h].astype(BF16) for h in heads]
            prod = [_dot(wb[h], jnp.concatenate([wb[h], zero_rows], axis=0)) for h in heads]
            wide = [prod[h] + jnp.where(left, 0.0, wide[h]) for h in heads]

        sol = []
        for h in heads:
            rhs = jnp.concatenate([vh[h] * b_col[h], kb[h] * eg_col[h]], axis=1).astype(BF16)
            sol.append(_dot(wide[h].astype(BF16),
                            jnp.concatenate([jnp.zeros_like(rhs), rhs], axis=0)))

        s_old = [s_ref[h] for h in heads]
        ws_qs = [_dot(jnp.concatenate([sol[h][:, HEAD_DIM:], qh[h] * eg_col[h]], axis=0).astype(BF16),
                      s_old[h].astype(BF16)) for h in heads]

        comb = []
        for h in heads:
            v_new = sol[h][:, :HEAD_DIM] - ws_qs[h][:CHUNK]
            vn2 = jnp.concatenate([v_new.astype(BF16), zero_rows], axis=0)
            kt = kh[h] * et_col[h]
            kt_t = jnp.concatenate([kt, jnp.zeros_like(kt)], axis=0).T
            comb.append(_dot(jnp.concatenate([attn[h], kt_t], axis=0).astype(BF16), vn2))

        for h in heads:
            gl2 = jnp.concatenate([gl_col[h], gl_col[h]], axis=0)
            s_ref[h] = s_old[h] * gl2 + comb[h][CHUNK:]
            o = ws_qs[h][CHUNK:] + comb[h][:CHUNK]
            on = o * lax.rsqrt(jnp.mean(o * o, axis=-1, keepdims=True) + EPS) * onw
            zh = z_ref[chains[h][0], rows, cols[h]]
            o_ref[chains[h][0], rows, cols[h]] = on * (zh * _sigmoid(zh))
        return carry

    lax.fori_loop(0, n_chunks, chunk_body, 0)


def _gdn_core(proj, conv_w, gate_par, o_norm_w, cs, bsz, seq):
    tb = GDN_BLOCK
    nb = GDN_BATCH if bsz % GDN_BATCH == 0 else 1
    return pl.pallas_call(
        _gdn_kernel,
        grid=(bsz // nb, seq // tb),
        in_specs=[
            pl.BlockSpec((nb, tb, WIDTH), lambda b, t: (b, t, 0)),
            pl.BlockSpec((nb, tb, WIDTH), lambda b, t: (b, t, 1)),
            pl.BlockSpec((nb, tb, WIDTH), lambda b, t: (b, t, 2)),
            pl.BlockSpec((nb, tb, WIDTH), lambda b, t: (b, t, 3)),
            pl.BlockSpec((nb, tb, LANES), lambda b, t: (b, t, 4 * WIDTH // LANES)),
            pl.BlockSpec((CONV_WIDTH, 3 * WIDTH), lambda b, t: (0, 0)),
            pl.BlockSpec((2, LANES), lambda b, t: (0, 0)),
            pl.BlockSpec((1, HEAD_DIM), lambda b, t: (0, 0)),
            pl.BlockSpec((tb, tb), lambda b, t: (0, 0)),
        ],
        out_specs=pl.BlockSpec((nb, tb, WIDTH), lambda b, t: (b, t, 0)),
        out_shape=jax.ShapeDtypeStruct((bsz, seq, WIDTH), F32),
        scratch_shapes=[
            pltpu.VMEM((nb, 3, SUBLANES, WIDTH), F32),
            pltpu.VMEM((nb * HEADS, HEAD_DIM, HEAD_DIM), F32),
            pltpu.VMEM((nb, tb, WIDTH), F32),
            pltpu.VMEM((nb, tb, WIDTH), F32),
            pltpu.VMEM((nb, tb, WIDTH), F32),
            pltpu.VMEM((nb, tb, LANES), F32),
            pltpu.VMEM((nb, tb, LANES), F32),
        ],
        compiler_params=pltpu.CompilerParams(
            dimension_semantics=("parallel", "arbitrary"), vmem_limit_bytes=VMEM_LIMIT),
        name="gdn_core",
    )(proj, proj, proj, proj, proj, conv_w, gate_par, o_norm_w.reshape(1, HEAD_DIM), cs)


def _sb_kernel(q_ref, k_ref, v_ref, gate_ref, mt_ref, o_ref, vt_ref, acc_ref, c_ref, za_ref, pb_ref):
    tq = q_ref.shape[1]
    seq = k_ref.shape[1]
    qi = pl.program_id(2)
    n_sub = tq // SB_SUB

    @pl.when(qi == 0)
    def _():
        def xpose(i, carry):
            r0 = pl.multiple_of(i * tq, tq)
            vt_ref[:, pl.ds(r0, tq)] = v_ref[0, pl.ds(r0, tq), :].astype(F32).T.astype(BF16)
            return carry
        lax.fori_loop(0, seq // tq, xpose, 0)

    q = q_ref[0]
    mt = mt_ref[...]
    acc_ref[...] = jnp.zeros_like(acc_ref)
    c_ref[...] = jnp.zeros_like(c_ref)

    def probabilities(z_sub, c, masked):
        probs = [None] * n_sub
        for jj in reversed(range(n_sub)):
            q0 = jj * SB_SUB if masked else 0
            zj = z_sub(jj)[:, q0:]
            cj = c[:, q0:]
            sp = jnp.maximum(zj, 0.0) + jnp.log(1.0 + jnp.exp2(-jnp.abs(zj))) * LOG2E
            if masked:
                mask = (lax.broadcasted_iota(jnp.int32, zj.shape, 0)
                        < lax.broadcasted_iota(jnp.int32, zj.shape, 1))
                sp = jnp.where(mask, sp, 0.0)
            t = _dot(mt, sp.astype(BF16))
            a = jnp.exp2(zj - sp - t - cj)
            if masked:
                a = jnp.where(mask, a, 0.0)
            a = a.astype(BF16)
            cj = cj + (t[0:1, :] + sp[0:1, :])
            if q0:
                a = jnp.concatenate([jnp.zeros((SB_SUB, q0), BF16), a], axis=1)
                cj = jnp.concatenate([c[:, :q0], cj], axis=1)
            probs[jj] = a
            c = cj
        return jnp.concatenate(probs, axis=0), c

    def start_of(g):
        return pl.multiple_of(g * tq, tq)

    def scores(g):
        return _dot_nt(k_ref[0, pl.ds(start_of(g), tq), :], q)

    def weighted_values(g, probs):
        return _dot(vt_ref[:, pl.ds(start_of(g), tq)], probs)

    def sub_of(z):
        return lambda jj: z[jj * SB_SUB:(jj + 1) * SB_SUB, :]

    odd = qi % 2
    n_pairs = qi // 2
    first_a = jnp.maximum(qi - 1 - odd, 0)

    def prologue(groups):
        c = c_ref[...]
        z = scores(groups[0][0])
        za_ref[...] = scores(first_a)
        for i, (g, masked) in enumerate(groups):
            probs, c = probabilities(sub_of(z), c, masked)
            if i + 1 < len(groups):
                z = scores(groups[i + 1][0])
                acc_ref[...] += weighted_values(g, probs)
            else:
                pb_ref[...] = probs
        c_ref[...] = c

    @pl.when(odd == 0)
    def _():
        prologue([(qi, True)])

    @pl.when(odd == 1)
    def _():
        prologue([(qi, True), (qi - 1, False)])

    def body(it, carry):
        ga = qi - 1 - odd - 2 * it
        gb = ga - 1
        av_prev = weighted_values(ga + 1, pb_ref[...])
        probs_a, c = probabilities(lambda jj: za_ref[jj * SB_SUB:(jj + 1) * SB_SUB, :], c_ref[...], False)
        zb = scores(gb)
        av_a = weighted_values(ga, probs_a)
        za_ref[...] = scores(jnp.maximum(ga - 2, 0))
        probs_b, c = probabilities(sub_of(zb), c, False)
        pb_ref[...] = probs_b
        c_ref[...] = c
        acc_ref[...] += av_prev + av_a
        return carry

    lax.fori_loop(0, n_pairs, body, 0)
    acc_ref[...] += weighted_values(0, pb_ref[...])
    gate = gate_ref[0]
    o_ref[0] = acc_ref[...].T * (gate * _sigmoid(gate))


def _sb_attention(qkv, gate, mo, bsz, seq):
    tq = SB_BLOCK
    return pl.pallas_call(
        _sb_kernel,
        grid=(bsz, HEADS, seq // tq),
        in_specs=[
            pl.BlockSpec((1, tq, HEAD_DIM), lambda b, h, i: (b, i, h)),
            pl.BlockSpec((1, seq, HEAD_DIM), lambda b, h, i: (b, 0, HEADS + h)),
            pl.BlockSpec((1, seq, HEAD_DIM), lambda b, h, i: (b, 0, 2 * HEADS + h)),
            pl.BlockSpec((1, tq, HEAD_DIM), lambda b, h, i: (b, i, h)),
            pl.BlockSpec((SB_SUB, SB_SUB), lambda b, h, i: (0, 0)),
        ],
        out_specs=pl.BlockSpec((1, tq, HEAD_DIM), lambda b, h, i: (b, i, h)),
        out_shape=jax.ShapeDtypeStruct((bsz, seq, WIDTH), F32),
        scratch_shapes=[
            pltpu.VMEM((HEAD_DIM, seq), BF16),
            pltpu.VMEM((HEAD_DIM, tq), F32),
            pltpu.VMEM((1, tq), F32),
            pltpu.VMEM((tq, tq), F32),
            pltpu.VMEM((tq, tq), BF16),
        ],
        compiler_params=pltpu.CompilerParams(
            dimension_semantics=("parallel", "parallel", "arbitrary"), vmem_limit_bytes=VMEM_LIMIT),
        name="sb_attention",
    )(qkv, qkv, qkv, gate, mo)


def _cumsum_matrix(tb):
    i = jnp.arange(tb)
    same_chunk = (i[:, None] // CHUNK) == (i[None, :] // CHUNK)
    return jnp.logical_and(same_chunk, i[:, None] >= i[None, :]).astype(BF16)


def _tail_matrix():
    s = jnp.arange(SB_SUB)[:, None]
    j = jnp.arange(SB_SUB)[None, :]
    return (j > s).astype(BF16)


def kernel(x, norm_w, a_w_in, a_conv_w, a_a_log, a_dt_bias, a_o_norm, a_w_out, b_w_in, b_w_out, final_norm_w):
    bsz, seq, d = x.shape
    m = bsz * seq
    depth = norm_w.shape[0]
    tm = min(1024, m)
    h2 = x.reshape(m, d)
    for i in range(depth):
        j = i // 2
        last = i == depth - 1
        if i % 2 == 0:
            wa = a_w_in[j]
            pad = jnp.zeros((d, LANES - 2 * HEADS), F32)
            w_all = jnp.concatenate([wa, pad], axis=1).astype(BF16)
            proj = _norm_proj(h2, norm_w[i], w_all, F32, tm, w_all.shape[1] // 3)
            gate_par = jnp.zeros((2, LANES), F32)
            gate_par = gate_par.at[0, HEADS:2 * HEADS].set(a_a_log[j])
            gate_par = gate_par.at[1, HEADS:2 * HEADS].set(a_dt_bias[j])
            og = _gdn_core(proj.reshape(bsz, seq, -1), a_conv_w[j], gate_par, a_o_norm[j],
                           _cumsum_matrix(GDN_BLOCK), bsz, seq)
            w_out = a_w_out[j]
        else:
            wb = b_w_in[j]
            w_qkv = jnp.concatenate([wb[:, :WIDTH] * (HEAD_DIM ** -0.5 * LOG2E), wb[:, WIDTH:3 * WIDTH]], axis=1)
            qkv = _norm_proj(h2, norm_w[i], w_qkv.astype(BF16), BF16, tm, WIDTH)
            gate = _norm_proj(h2, norm_w[i], wb[:, 3 * WIDTH:].astype(BF16), F32, tm, WIDTH)
            og = _sb_attention(qkv.reshape(bsz, seq, -1), gate.reshape(bsz, seq, -1),
                               _tail_matrix(), bsz, seq)
            w_out = b_w_out[j]
        h2 = _out_proj(og.reshape(m, WIDTH), w_out.astype(BF16), h2, final_norm_w, last, min(512, m))
    return h2.reshape(bsz, seq, d)
```

```python
import jax
import jax.numpy as jnp
from jax import lax
from jax.experimental import pallas as pl
from jax.experimental.pallas import tpu as pltpu

EPS = 1e-6
HEADS = 8
HEAD_DIM = 128
WIDTH = HEADS * HEAD_DIM
CHUNK = 64
CONV_WIDTH = 4
LANES = 128
SUBLANES = 8
GDN_BLOCK = 256
GDN_BATCH = 2
SB_BLOCK = 512
LOG2E = 1.4426950408889634
SB_SUB = 128
VMEM_LIMIT = 48 * 1024 * 1024

F32 = jnp.float32
BF16 = jnp.bfloat16


def _sigmoid(x):
    return 1.0 / (1.0 + jnp.exp2(x * (-LOG2E)))


def _softplus(x):
    return jnp.maximum(x, 0.0) + jnp.log(1.0 + jnp.exp(-jnp.abs(x)))


def _dot(a, b):
    return jnp.dot(a, b, preferred_element_type=F32)


def _dot_nt(a, b):
    return lax.dot_general(a, b, (((1,), (1,)), ((), ())), preferred_element_type=F32)


def _norm_proj_kernel(x_ref, nw_ref, w_ref, o_ref, xn_ref):
    @pl.when(pl.program_id(1) == 0)
    def _():
        x = x_ref[...]
        ms = jnp.mean(x * x, axis=-1, keepdims=True)
        xn_ref[...] = (x * lax.rsqrt(ms + EPS) * nw_ref[...]).astype(BF16)

    o_ref[...] = _dot(xn_ref[...], w_ref[...]).astype(o_ref.dtype)


def _norm_proj(x2, nw, w, out_dtype, tm, tn):
    m, d = x2.shape
    n = w.shape[1]
    return pl.pallas_call(
        _norm_proj_kernel,
        grid=(m // tm, n // tn),
        in_specs=[
            pl.BlockSpec((tm, d), lambda i, j: (i, 0)),
            pl.BlockSpec((1, d), lambda i, j: (0, 0)),
            pl.BlockSpec((d, tn), lambda i, j: (0, j)),
        ],
        out_specs=pl.BlockSpec((tm, tn), lambda i, j: (i, j)),
        out_shape=jax.ShapeDtypeStruct((m, n), out_dtype),
        scratch_shapes=[pltpu.VMEM((tm, d), BF16)],
        compiler_params=pltpu.CompilerParams(
            dimension_semantics=("parallel", "arbitrary"), vmem_limit_bytes=VMEM_LIMIT),
        name="norm_proj",
    )(x2, nw.reshape(1, d), w)


def _out_norm_proj_kernel(a_ref, wo_ref, r_ref, nw_ref, w_ref, h_ref, o_ref, xn_ref):
    @pl.when(pl.program_id(1) == 0)
    def _():
        h = r_ref[...] + _dot(a_ref[...].astype(BF16), wo_ref[...])
        h_ref[...] = h
        ms = jnp.mean(h * h, axis=-1, keepdims=True)
        xn_ref[...] = (h * lax.rsqrt(ms + EPS) * nw_ref[...]).astype(BF16)

    o_ref[...] = _dot(xn_ref[...], w_ref[...]).astype(o_ref.dtype)


def _out_norm_proj(a2, wo, res2, nw, w, out_dtype, tm, tn):
    m, d = res2.shape
    n = w.shape[1]
    return pl.pallas_call(
        _out_norm_proj_kernel,
        grid=(m // tm, n // tn),
        in_specs=[
            pl.BlockSpec((tm, a2.shape[1]), lambda i, j: (i, 0)),
            pl.BlockSpec(wo.shape, lambda i, j: (0, 0)),
            pl.BlockSpec((tm, d), lambda i, j: (i, 0)),
            pl.BlockSpec((1, d), lambda i, j: (0, 0)),
            pl.BlockSpec((d, tn), lambda i, j: (0, j)),
        ],
        out_specs=[
            pl.BlockSpec((tm, d), lambda i, j: (i, 0)),
            pl.BlockSpec((tm, tn), lambda i, j: (i, j)),
        ],
        out_shape=[jax.ShapeDtypeStruct((m, d), F32), jax.ShapeDtypeStruct((m, n), out_dtype)],
        scratch_shapes=[pltpu.VMEM((tm, d), BF16)],
        compiler_params=pltpu.CompilerParams(
            dimension_semantics=("parallel", "arbitrary"), vmem_limit_bytes=VMEM_LIMIT),
        name="out_norm_proj",
    )(a2, wo, res2, nw.reshape(1, d), w)


def _out_proj_kernel(a_ref, w_ref, r_ref, fw_ref, o_ref):
    y = r_ref[...] + _dot(a_ref[...].astype(BF16), w_ref[...])
    ms = jnp.mean(y * y, axis=-1, keepdims=True)
    o_ref[...] = y * lax.rsqrt(ms + EPS) * fw_ref[...]


def _out_proj(a2, w, res2, fw, tm):
    m, d = a2.shape
    n = w.shape[1]
    return pl.pallas_call(
        _out_proj_kernel,
        grid=(m // tm,),
        in_specs=[
            pl.BlockSpec((tm, d), lambda i: (i, 0)),
            pl.BlockSpec((d, n), lambda i: (0, 0)),
            pl.BlockSpec((tm, n), lambda i: (i, 0)),
            pl.BlockSpec((1, n), lambda i: (0, 0)),
        ],
        out_specs=pl.BlockSpec((tm, n), lambda i: (i, 0)),
        out_shape=jax.ShapeDtypeStruct((m, n), F32),
        compiler_params=pltpu.CompilerParams(
            dimension_semantics=("parallel",), vmem_limit_bytes=VMEM_LIMIT),
        name="out_proj",
    )(a2, w, res2, fw.reshape(1, n))


def _col(x, lane_idx, lane):
    return jnp.sum(jnp.where(lane == lane_idx, x, 0.0), axis=-1, keepdims=True)


def _split3(x):
    h1 = x.astype(BF16)
    r1 = x - h1.astype(F32)
    h2 = r1.astype(BF16)
    h3 = (r1 - h2.astype(F32)).astype(BF16)
    return h1, h2, h3


def _gdn_kernel(q_ref, k_ref, v_ref, z_ref, bg_ref, cw_ref, gp_ref, onw_ref, cs_ref,
                o_ref, tail_ref, s_ref, qn_ref, kn_ref, vn_ref, gc_ref, beta_ref):
    nb, tb = q_ref.shape[0], q_ref.shape[1]
    n_chunks = tb // CHUNK

    @pl.when(pl.program_id(1) == 0)
    def _():
        tail_ref[...] = jnp.zeros_like(tail_ref)
        s_ref[...] = jnp.zeros_like(s_ref)

    cs = cs_ref[...]
    scale = HEAD_DIM ** -0.5
    for bi in range(nb):
        bg = bg_ref[bi]
        beta_ref[bi] = _sigmoid(bg)
        g = -jnp.exp(gp_ref[0:1, :]) * _softplus(bg + gp_ref[1:2, :])
        g1, g2, g3 = _split3(g)
        gc_ref[bi] = _dot(cs, g1) + _dot(cs, g2) + _dot(cs, g3)

        for a, (src, dst) in enumerate(((q_ref, qn_ref), (k_ref, kn_ref), (v_ref, vn_ref))):
            for h in range(HEADS):
                cols = slice(h * HEAD_DIM, (h + 1) * HEAD_DIM)
                wcols = slice(a * WIDTH + h * HEAD_DIM, a * WIDTH + (h + 1) * HEAD_DIM)
                x = src[bi, :, cols]
                xp = jnp.concatenate([tail_ref[bi, a, :, cols], x], axis=0)
                y = x * cw_ref[3:4, wcols]
                for s in range(1, CONV_WIDTH):
                    shifted = pltpu.roll(xp, s, axis=0)[SUBLANES:]
                    y = y + shifted * cw_ref[3 - s:4 - s, wcols]
                tail_ref[bi, a, :, cols] = x[tb - SUBLANES:]
                y = y * _sigmoid(y)
                if a < 2:
                    y = y * lax.rsqrt(jnp.sum(y * y, axis=-1, keepdims=True) + EPS)
                    if a == 0:
                        y = y * scale
                dst[bi, :, cols] = y

    lane = lax.broadcasted_iota(jnp.int32, (CHUNK, LANES), 1)
    row = lax.broadcasted_iota(jnp.int32, (CHUNK, LANES), 0)
    lane_t = lane & (CHUNK - 1)
    left = lane < CHUNK
    causal = row >= lane_t
    w_init = jnp.where(left, 0.0, jnp.where(row == lane_t, 1.0, 0.0))
    strict_left = jnp.logical_and(left, row > lane_t)
    onw = onw_ref[...]

    def chunk_body(c, carry):
        r0 = pl.multiple_of(c * CHUNK, CHUNK)
        rows = pl.ds(r0, CHUNK)
        chains = [(bi, hh) for bi in range(nb) for hh in range(HEADS)]
        heads = range(len(chains))
        cols = [slice(hh * HEAD_DIM, (hh + 1) * HEAD_DIM) for _, hh in chains]
        b_col, g_col, eg_col, et_col, gl_col, g_row = [], [], [], [], [], []
        for bi in range(nb):
            gc = gc_ref[bi, rows, :]
            beta = beta_ref[bi, rows, :]
            g_last = jnp.broadcast_to(gc[CHUNK - 1:CHUNK, :], (CHUNK, LANES))
            eg = jnp.exp(gc)
            et = jnp.exp(g_last - gc)
            egl = jnp.exp(g_last)
            gct = jnp.concatenate([gc, gc], axis=0).T
            for hh in range(HEADS):
                b_col.append(_col(beta, hh, lane))
                g_col.append(_col(gc, HEADS + hh, lane))
                eg_col.append(_col(eg, HEADS + hh, lane))
                et_col.append(_col(et, HEADS + hh, lane))
                gl_col.append(_col(egl, HEADS + hh, lane))
                g_row.append(gct[HEADS + hh:HEADS + hh + 1, :])
        qh = [qn_ref[chains[h][0], rows, cols[h]] for h in heads]
        kh = [kn_ref[chains[h][0], rows, cols[h]] for h in heads]
        vh = [vn_ref[chains[h][0], rows, cols[h]] for h in heads]
        decay = [jnp.exp(jnp.where(causal, g_col[h] - g_row[h], -jnp.inf)) for h in heads]
        kb = [kh[h] * b_col[h] for h in heads]

        kq = [_dot_nt(jnp.concatenate([kb[h], qh[h]], axis=0).astype(BF16),
                      jnp.concatenate([kh[h], kh[h]], axis=0).astype(BF16)) for h in heads]
        attn = [kq[h][CHUNK:] * decay[h] for h in heads]

        wide = [jnp.where(strict_left, -(kq[h][:CHUNK] * decay[h]), w_init) for h in heads]
        zero_rows = jnp.zeros((CHUNK, LANES), BF16)
        for _ in range(6):
            wb = [wide[h].astype(BF16) for h in heads]
            prod = [_dot(wb[h], jnp.concatenate([wb[h], zero_rows], axis=0)) for h in heads]
            wide = [prod[h] + jnp.where(left, 0.0, wide[h]) for h in heads]

        sol = []
        for h in heads:
            rhs = jnp.concatenate([vh[h] * b_col[h], kb[h] * eg_col[h]], axis=1).astype(BF16)
            sol.append(_dot(wide[h].astype(BF16),
                            jnp.concatenate([jnp.zeros_like(rhs), rhs], axis=0)))

        s_old = [s_ref[h] for h in heads]
        ws_qs = [_dot(jnp.concatenate([sol[h][:, HEAD_DIM:], qh[h] * eg_col[h]], axis=0).astype(BF16),
                      s_old[h].astype(BF16)) for h in heads]

        comb = []
        for h in heads:
            v_new = sol[h][:, :HEAD_DIM] - ws_qs[h][:CHUNK]
            vn2 = jnp.concatenate([v_new.astype(BF16), zero_rows], axis=0)
            kt = kh[h] * et_col[h]
            kt_t = jnp.concatenate([kt, jnp.zeros_like(kt)], axis=0).T
            comb.append(_dot(jnp.concatenate([attn[h], kt_t], axis=0).astype(BF16), vn2))

        for h in heads:
            gl2 = jnp.concatenate([gl_col[h], gl_col[h]], axis=0)
            s_ref[h] = s_old[h] * gl2 + comb[h][CHUNK:]
            o = ws_qs[h][CHUNK:] + comb[h][:CHUNK]
            on = o * lax.rsqrt(jnp.mean(o * o, axis=-1, keepdims=True) + EPS) * onw
            zh = z_ref[chains[h][0], rows, cols[h]]
            o_ref[chains[h][0], rows, cols[h]] = on * (zh * _sigmoid(zh))
        return carry

    lax.fori_loop(0, n_chunks, chunk_body, 0)


def _gdn_core(proj, conv_w, gate_par, o_norm_w, cs, bsz, seq):
    tb = GDN_BLOCK
    nb = GDN_BATCH if bsz % GDN_BATCH == 0 else 1
    return pl.pallas_call(
        _gdn_kernel,
        grid=(bsz // nb, seq // tb),
        in_specs=[
            pl.BlockSpec((nb, tb, WIDTH), lambda b, t: (b, t, 0)),
            pl.BlockSpec((nb, tb, WIDTH), lambda b, t: (b, t, 1)),
            pl.BlockSpec((nb, tb, WIDTH), lambda b, t: (b, t, 2)),
            pl.BlockSpec((nb, tb, WIDTH), lambda b, t: (b, t, 3)),
            pl.BlockSpec((nb, tb, LANES), lambda b, t: (b, t, 4 * WIDTH // LANES)),
            pl.BlockSpec((CONV_WIDTH, 3 * WIDTH), lambda b, t: (0, 0)),
            pl.BlockSpec((2, LANES), lambda b, t: (0, 0)),
            pl.BlockSpec((1, HEAD_DIM), lambda b, t: (0, 0)),
            pl.BlockSpec((tb, tb), lambda b, t: (0, 0)),
        ],
        out_specs=pl.BlockSpec((nb, tb, WIDTH), lambda b, t: (b, t, 0)),
        out_shape=jax.ShapeDtypeStruct((bsz, seq, WIDTH), F32),
        scratch_shapes=[
            pltpu.VMEM((nb, 3, SUBLANES, WIDTH), F32),
            pltpu.VMEM((nb * HEADS, HEAD_DIM, HEAD_DIM), F32),
            pltpu.VMEM((nb, tb, WIDTH), F32),
            pltpu.VMEM((nb, tb, WIDTH), F32),
            pltpu.VMEM((nb, tb, WIDTH), F32),
            pltpu.VMEM((nb, tb, LANES), F32),
            pltpu.VMEM((nb, tb, LANES), F32),
        ],
        compiler_params=pltpu.CompilerParams(
            dimension_semantics=("parallel", "arbitrary"), vmem_limit_bytes=VMEM_LIMIT),
        name="gdn_core",
    )(proj, proj, proj, proj, proj, conv_w, gate_par, o_norm_w.reshape(1, HEAD_DIM), cs)


def _sb_kernel(q_ref, k_ref, v_ref, gate_ref, mt_ref, o_ref, vt_ref, acc_ref, c_ref, za_ref, pb_ref):
    tq = q_ref.shape[1]
    seq = k_ref.shape[1]
    qi = pl.program_id(2)
    n_sub = tq // SB_SUB

    @pl.when(qi == 0)
    def _():
        def xpose(i, carry):
            r0 = pl.multiple_of(i * tq, tq)
            vt_ref[:, pl.ds(r0, tq)] = v_ref[0, pl.ds(r0, tq), :].astype(F32).T.astype(BF16)
            return carry
        lax.fori_loop(0, seq // tq, xpose, 0)

    q = q_ref[0]
    mt = mt_ref[...]
    acc_ref[...] = jnp.zeros_like(acc_ref)
    c_ref[...] = jnp.zeros_like(c_ref)

    def probabilities(z_sub, c, masked):
        probs = [None] * n_sub
        for jj in reversed(range(n_sub)):
            q0 = jj * SB_SUB if masked else 0
            zj = z_sub(jj)[:, q0:]
            cj = c[:, q0:]
            sp = jnp.maximum(zj, 0.0) + jnp.log(1.0 + jnp.exp2(-jnp.abs(zj))) * LOG2E
            if masked:
                mask = (lax.broadcasted_iota(jnp.int32, zj.shape, 0)
                        < lax.broadcasted_iota(jnp.int32, zj.shape, 1))
                sp = jnp.where(mask, sp, 0.0)
            t = _dot(mt, sp.astype(BF16))
            a = jnp.exp2(zj - sp - t - cj)
            if masked:
                a = jnp.where(mask, a, 0.0)
            a = a.astype(BF16)
            cj = cj + (t[0:1, :] + sp[0:1, :])
            if q0:
                a = jnp.concatenate([jnp.zeros((SB_SUB, q0), BF16), a], axis=1)
                cj = jnp.concatenate([c[:, :q0], cj], axis=1)
            probs[jj] = a
            c = cj
        return jnp.concatenate(probs, axis=0), c

    def start_of(g):
        return pl.multiple_of(g * tq, tq)

    def scores(g):
        return _dot_nt(k_ref[0, pl.ds(start_of(g), tq), :], q)

    def weighted_values(g, probs):
        return _dot(vt_ref[:, pl.ds(start_of(g), tq)], probs)

    def sub_of(z):
        return lambda jj: z[jj * SB_SUB:(jj + 1) * SB_SUB, :]

    odd = qi % 2
    n_pairs = qi // 2
    first_a = jnp.maximum(qi - 1 - odd, 0)

    def prologue(groups):
        c = c_ref[...]
        z = scores(groups[0][0])
        za_ref[...] = scores(first_a)
        for i, (g, masked) in enumerate(groups):
            probs, c = probabilities(sub_of(z), c, masked)
            if i + 1 < len(groups):
                z = scores(groups[i + 1][0])
                acc_ref[...] += weighted_values(g, probs)
            else:
                pb_ref[...] = probs
        c_ref[...] = c

    @pl.when(odd == 0)
    def _():
        prologue([(qi, True)])

    @pl.when(odd == 1)
    def _():
        prologue([(qi, True), (qi - 1, False)])

    def body(it, carry):
        ga = qi - 1 - odd - 2 * it
        gb = ga - 1
        av_prev = weighted_values(ga + 1, pb_ref[...])
        probs_a, c = probabilities(lambda jj: za_ref[jj * SB_SUB:(jj + 1) * SB_SUB, :], c_ref[...], False)
        zb = scores(gb)
        av_a = weighted_values(ga, probs_a)
        za_ref[...] = scores(jnp.maximum(ga - 2, 0))
        probs_b, c = probabilities(sub_of(zb), c, False)
        pb_ref[...] = probs_b
        c_ref[...] = c
        acc_ref[...] += av_prev + av_a
        return carry

    lax.fori_loop(0, n_pairs, body, 0)
    acc_ref[...] += weighted_values(0, pb_ref[...])
    gate = gate_ref[0]
    o_ref[0] = acc_ref[...].T * (gate * _sigmoid(gate))


def _sb_attention(qkv, gate, mo, bsz, seq):
    tq = SB_BLOCK
    return pl.pallas_call(
        _sb_kernel,
        grid=(bsz, HEADS, seq // tq),
        in_specs=[
            pl.BlockSpec((1, tq, HEAD_DIM), lambda b, h, i: (b, i, h)),
            pl.BlockSpec((1, seq, HEAD_DIM), lambda b, h, i: (b, 0, HEADS + h)),
            pl.BlockSpec((1, seq, HEAD_DIM), lambda b, h, i: (b, 0, 2 * HEADS + h)),
            pl.BlockSpec((1, tq, HEAD_DIM), lambda b, h, i: (b, i, h)),
            pl.BlockSpec((SB_SUB, SB_SUB), lambda b, h, i: (0, 0)),
        ],
        out_specs=pl.BlockSpec((1, tq, HEAD_DIM), lambda b, h, i: (b, i, h)),
        out_shape=jax.ShapeDtypeStruct((bsz, seq, WIDTH), F32),
        scratch_shapes=[
            pltpu.VMEM((HEAD_DIM, seq), BF16),
            pltpu.VMEM((HEAD_DIM, tq), F32),
            pltpu.VMEM((1, tq), F32),
            pltpu.VMEM((tq, tq), F32),
            pltpu.VMEM((tq, tq), BF16),
        ],
        compiler_params=pltpu.CompilerParams(
            dimension_semantics=("parallel", "parallel", "arbitrary"), vmem_limit_bytes=VMEM_LIMIT),
        name="sb_attention",
    )(qkv, qkv, qkv, gate, mo)


def _cumsum_matrix(tb):
    i = jnp.arange(tb)
    same_chunk = (i[:, None] // CHUNK) == (i[None, :] // CHUNK)
    return jnp.logical_and(same_chunk, i[:, None] >= i[None, :]).astype(BF16)


def _tail_matrix():
    s = jnp.arange(SB_SUB)[:, None]
    j = jnp.arange(SB_SUB)[None, :]
    return (j > s).astype(BF16)


def kernel(x, norm_w, a_w_in, a_conv_w, a_a_log, a_dt_bias, a_o_norm, a_w_out, b_w_in, b_w_out, final_norm_w):
    bsz, seq, d = x.shape
    m = bsz * seq
    depth = norm_w.shape[0]
    tm = min(1024, m)
    tm_out = min(512, m)
    h2 = x.reshape(m, d)
    pending = None

    def project(h2, nw, w, out_dtype, tn):
        if pending is None:
            return h2, _norm_proj(h2, nw, w, out_dtype, tm, tn)
        return _out_norm_proj(pending[0], pending[1], h2, nw, w, out_dtype, tm_out, tn)

    for i in range(depth):
        j = i // 2
        if i % 2 == 0:
            wa = a_w_in[j]
            pad = jnp.zeros((d, LANES - 2 * HEADS), F32)
            w_all = jnp.concatenate([wa, pad], axis=1).astype(BF16)
            h2, proj = project(h2, norm_w[i], w_all, F32, w_all.shape[1] // 3)
            gate_par = jnp.zeros((2, LANES), F32)
            gate_par = gate_par.at[0, HEADS:2 * HEADS].set(a_a_log[j])
            gate_par = gate_par.at[1, HEADS:2 * HEADS].set(a_dt_bias[j])
            og = _gdn_core(proj.reshape(bsz, seq, -1), a_conv_w[j], gate_par, a_o_norm[j],
                           _cumsum_matrix(GDN_BLOCK), bsz, seq)
            w_out = a_w_out[j]
        else:
            wb = b_w_in[j]
            w_qkv = jnp.concatenate([wb[:, :WIDTH] * (HEAD_DIM ** -0.5 * LOG2E), wb[:, WIDTH:3 * WIDTH]], axis=1)
            h2, qkv = project(h2, norm_w[i], w_qkv.astype(BF16), BF16, WIDTH)
            gate = _norm_proj(h2, norm_w[i], wb[:, 3 * WIDTH:].astype(BF16), F32, tm, WIDTH)
            og = _sb_attention(qkv.reshape(bsz, seq, -1), gate.reshape(bsz, seq, -1),
                               _tail_matrix(), bsz, seq)
            w_out = b_w_out[j]
        pending = (og.reshape(m, WIDTH), w_out.astype(BF16))
    out = _out_proj(pending[0], pending[1], h2, final_norm_w, tm_out)
    return out.reshape(bsz, seq, d)
```

```python
import functools

import jax
import jax.numpy as jnp
from jax import lax
from jax.experimental import pallas as pl
from jax.experimental.pallas import tpu as pltpu

EPS = 1e-6
HEADS = 8
HEAD_DIM = 128
WIDTH = HEADS * HEAD_DIM
CHUNK = 64
CONV_WIDTH = 4
LANES = 128
SUBLANES = 8
GDN_BLOCK = 256
GDN_BATCH = 2
SB_BLOCK = 512
LOG2E = 1.4426950408889634
SB_SUB = 128
VMEM_LIMIT = 48 * 1024 * 1024

F32 = jnp.float32
BF16 = jnp.bfloat16


def _sigmoid(x):
    return 1.0 / (1.0 + jnp.exp2(x * (-LOG2E)))


def _softplus(x):
    return jnp.maximum(x, 0.0) + jnp.log(1.0 + jnp.exp(-jnp.abs(x)))


def _dot(a, b):
    return jnp.dot(a, b, preferred_element_type=F32)


def _dot_nt(a, b):
    return lax.dot_general(a, b, (((1,), (1,)), ((), ())), preferred_element_type=F32)


def _norm_proj_kernel(x_ref, nw_ref, w_ref, o_ref, xn_ref):
    @pl.when(pl.program_id(1) == 0)
    def _():
        x = x_ref[...]
        ms = jnp.mean(x * x, axis=-1, keepdims=True)
        xn_ref[...] = (x * lax.rsqrt(ms + EPS) * nw_ref[...]).astype(BF16)

    o_ref[...] = _dot(xn_ref[...], w_ref[...]).astype(o_ref.dtype)


def _norm_proj(x2, nw, w, out_dtype, tm, tn):
    m, d = x2.shape
    n = w.shape[1]
    return pl.pallas_call(
        _norm_proj_kernel,
        grid=(m // tm, n // tn),
        in_specs=[
            pl.BlockSpec((tm, d), lambda i, j: (i, 0)),
            pl.BlockSpec((1, d), lambda i, j: (0, 0)),
            pl.BlockSpec((d, tn), lambda i, j: (0, j)),
        ],
        out_specs=pl.BlockSpec((tm, tn), lambda i, j: (i, j)),
        out_shape=jax.ShapeDtypeStruct((m, n), out_dtype),
        scratch_shapes=[pltpu.VMEM((tm, d), BF16)],
        compiler_params=pltpu.CompilerParams(
            dimension_semantics=("parallel", "arbitrary"), vmem_limit_bytes=VMEM_LIMIT),
        name="norm_proj",
    )(x2, nw.reshape(1, d), w)


def _out_proj_kernel(a_ref, w_ref, r_ref, fw_ref, o_ref, *, final_norm):
    y = r_ref[...] + _dot(a_ref[...].astype(BF16), w_ref[...])
    if final_norm:
        ms = jnp.mean(y * y, axis=-1, keepdims=True)
        y = y * lax.rsqrt(ms + EPS) * fw_ref[...]
    o_ref[...] = y


def _out_proj(a2, w, res2, fw, final_norm, tm):
    m, d = a2.shape
    n = w.shape[1]
    return pl.pallas_call(
        functools.partial(_out_proj_kernel, final_norm=final_norm),
        grid=(m // tm,),
        in_specs=[
            pl.BlockSpec((tm, d), lambda i: (i, 0)),
            pl.BlockSpec((d, n), lambda i: (0, 0)),
            pl.BlockSpec((tm, n), lambda i: (i, 0)),
            pl.BlockSpec((1, n), lambda i: (0, 0)),
        ],
        out_specs=pl.BlockSpec((tm, n), lambda i: (i, 0)),
        out_shape=jax.ShapeDtypeStruct((m, n), F32),
        compiler_params=pltpu.CompilerParams(
            dimension_semantics=("parallel",), vmem_limit_bytes=VMEM_LIMIT),
        name="out_proj",
    )(a2, w, res2, fw.reshape(1, n))


def _col(x, lane_idx, lane):
    return jnp.sum(jnp.where(lane == lane_idx, x, 0.0), axis=-1, keepdims=True)


def _split3(x):
    h1 = x.astype(BF16)
    r1 = x - h1.astype(F32)
    h2 = r1.astype(BF16)
    h3 = (r1 - h2.astype(F32)).astype(BF16)
    return h1, h2, h3


def _gdn_kernel(q_ref, k_ref, v_ref, z_ref, bg_ref, cw_ref, gp_ref, onw_ref, cs_ref,
                o_ref, tail_ref, s_ref, qn_ref, kn_ref, vn_ref, gc_ref, beta_ref):
    nb, tb = q_ref.shape[0], q_ref.shape[1]
    n_chunks = tb // CHUNK

    @pl.when(pl.program_id(1) == 0)
    def _():
        tail_ref[...] = jnp.zeros_like(tail_ref)
        s_ref[...] = jnp.zeros_like(s_ref)

    cs = cs_ref[...]
    scale = HEAD_DIM ** -0.5
    for bi in range(nb):
        bg = bg_ref[bi]
        beta_ref[bi] = _sigmoid(bg)
        g = -jnp.exp(gp_ref[0:1, :]) * _softplus(bg + gp_ref[1:2, :])
        g1, g2, g3 = _split3(g)
        gc_ref[bi] = _dot(cs, g1) + _dot(cs, g2) + _dot(cs, g3)

        for a, (src, dst) in enumerate(((q_ref, qn_ref), (k_ref, kn_ref), (v_ref, vn_ref))):
            for h in range(HEADS):
                cols = slice(h * HEAD_DIM, (h + 1) * HEAD_DIM)
                wcols = slice(a * WIDTH + h * HEAD_DIM, a * WIDTH + (h + 1) * HEAD_DIM)
                x = src[bi, :, cols]
                xp = jnp.concatenate([tail_ref[bi, a, :, cols], x], axis=0)
                y = x * cw_ref[3:4, wcols]
                for s in range(1, CONV_WIDTH):
                    shifted = pltpu.roll(xp, s, axis=0)[SUBLANES:]
                    y = y + shifted * cw_ref[3 - s:4 - s, wcols]
                tail_ref[bi, a, :, cols] = x[tb - SUBLANES:]
                y = y * _sigmoid(y)
                if a < 2:
                    y = y * lax.rsqrt(jnp.sum(y * y, axis=-1, keepdims=True) + EPS)
                    if a == 0:
                        y = y * scale
                dst[bi, :, cols] = y

    lane = lax.broadcasted_iota(jnp.int32, (CHUNK, LANES), 1)
    row = lax.broadcasted_iota(jnp.int32, (CHUNK, LANES), 0)
    lane_t = lane & (CHUNK - 1)
    left = lane < CHUNK
    causal = row >= lane_t
    w_init = jnp.where(left, 0.0, jnp.where(row == lane_t, 1.0, 0.0))
    strict_left = jnp.logical_and(left, row > lane_t)
    onw = onw_ref[...]

    def chunk_body(c, carry):
        r0 = pl.multiple_of(c * CHUNK, CHUNK)
        rows = pl.ds(r0, CHUNK)
        chains = [(bi, hh) for bi in range(nb) for hh in range(HEADS)]
        heads = range(len(chains))
        cols = [slice(hh * HEAD_DIM, (hh + 1) * HEAD_DIM) for _, hh in chains]
        b_col, g_col, eg_col, et_col, gl_col, g_row = [], [], [], [], [], []
        for bi in range(nb):
            gc = gc_ref[bi, rows, :]
            beta = beta_ref[bi, rows, :]
            g_last = jnp.broadcast_to(gc[CHUNK - 1:CHUNK, :], (CHUNK, LANES))
            eg = jnp.exp(gc)
            et = jnp.exp(g_last - gc)
            egl = jnp.exp(g_last)
            gct = jnp.concatenate([gc, gc], axis=0).T
            for hh in range(HEADS):
                b_col.append(_col(beta, hh, lane))
                g_col.append(_col(gc, HEADS + hh, lane))
                eg_col.append(_col(eg, HEADS + hh, lane))
                et_col.append(_col(et, HEADS + hh, lane))
                gl_col.append(_col(egl, HEADS + hh, lane))
                g_row.append(gct[HEADS + hh:HEADS + hh + 1, :])
        qh = [qn_ref[chains[h][0], rows, cols[h]] for h in heads]
        kh = [kn_ref[chains[h][0], rows, cols[h]] for h in heads]
        vh = [vn_ref[chains[h][0], rows, cols[h]] for h in heads]
        decay = [jnp.exp(jnp.where(causal, g_col[h] - g_row[h], -jnp.inf)) for h in heads]
        kb = [kh[h] * b_col[h] for h in heads]

        kq = [_dot_nt(jnp.concatenate([kb[h], qh[h]], axis=0).astype(BF16),
                      jnp.concatenate([kh[h], kh[h]], axis=0).astype(BF16)) for h in heads]
        attn = [kq[h][CHUNK:] * decay[h] for h in heads]

        wide = [jnp.where(strict_left, -(kq[h][:CHUNK] * decay[h]), w_init) for h in heads]
        zero_rows = jnp.zeros((CHUNK, LANES), BF16)
        for _ in range(6):
            wb = [wide[h].astype(BF16) for h in heads]
            prod = [_dot(wb[h], jnp.concatenate([wb[h], zero_rows], axis=0)) for h in heads]
            wide = [prod[h] + jnp.where(left, 0.0, wide[h]) for h in heads]

        sol = []
        for h in heads:
            rhs = jnp.concatenate([vh[h] * b_col[h], kb[h] * eg_col[h]], axis=1).astype(BF16)
            sol.append(_dot(wide[h].astype(BF16),
                            jnp.concatenate([jnp.zeros_like(rhs), rhs], axis=0)))

        s_old = [s_ref[h] for h in heads]
        ws_qs = [_dot(jnp.concatenate([sol[h][:, HEAD_DIM:], qh[h] * eg_col[h]], axis=0).astype(BF16),
                      s_old[h].astype(BF16)) for h in heads]

        comb = []
        for h in heads:
            v_new = sol[h][:, :HEAD_DIM] - ws_qs[h][:CHUNK]
            vn2 = jnp.concatenate([v_new.astype(BF16), zero_rows], axis=0)
            kt = kh[h] * et_col[h]
            kt_t = jnp.concatenate([kt, jnp.zeros_like(kt)], axis=0).T
            comb.append(_dot(jnp.concatenate([attn[h], kt_t], axis=0).astype(BF16), vn2))

        for h in heads:
            gl2 = jnp.concatenate([gl_col[h], gl_col[h]], axis=0)
            s_ref[h] = s_old[h] * gl2 + comb[h][CHUNK:]
            o = ws_qs[h][CHUNK:] + comb[h][:CHUNK]
            on = o * lax.rsqrt(jnp.mean(o * o, axis=-1, keepdims=True) + EPS) * onw
            zh = z_ref[chains[h][0], rows, cols[h]]
            o_ref[chains[h][0], rows, cols[h]] = on * (zh * _sigmoid(zh))
        return carry

    lax.fori_loop(0, n_chunks, chunk_body, 0)


def _gdn_core(proj, conv_w, gate_par, o_norm_w, cs, bsz, seq):
    tb = GDN_BLOCK
    nb = GDN_BATCH if bsz % GDN_BATCH == 0 else 1
    return pl.pallas_call(
        _gdn_kernel,
        grid=(bsz // nb, seq // tb),
        in_specs=[
            pl.BlockSpec((nb, tb, WIDTH), lambda b, t: (b, t, 0)),
            pl.BlockSpec((nb, tb, WIDTH), lambda b, t: (b, t, 1)),
            pl.BlockSpec((nb, tb, WIDTH), lambda b, t: (b, t, 2)),
            pl.BlockSpec((nb, tb, WIDTH), lambda b, t: (b, t, 3)),
            pl.BlockSpec((nb, tb, LANES), lambda b, t: (b, t, 4 * WIDTH // LANES)),
            pl.BlockSpec((CONV_WIDTH, 3 * WIDTH), lambda b, t: (0, 0)),
            pl.BlockSpec((2, LANES), lambda b, t: (0, 0)),
            pl.BlockSpec((1, HEAD_DIM), lambda b, t: (0, 0)),
            pl.BlockSpec((tb, tb), lambda b, t: (0, 0)),
        ],
        out_specs=pl.BlockSpec((nb, tb, WIDTH), lambda b, t: (b, t, 0)),
        out_shape=jax.ShapeDtypeStruct((bsz, seq, WIDTH), F32),
        scratch_shapes=[
            pltpu.VMEM((nb, 3, SUBLANES, WIDTH), F32),
            pltpu.VMEM((nb * HEADS, HEAD_DIM, HEAD_DIM), F32),
            pltpu.VMEM((nb, tb, WIDTH), F32),
            pltpu.VMEM((nb, tb, WIDTH), F32),
            pltpu.VMEM((nb, tb, WIDTH), F32),
            pltpu.VMEM((nb, tb, LANES), F32),
            pltpu.VMEM((nb, tb, LANES), F32),
        ],
        compiler_params=pltpu.CompilerParams(
            dimension_semantics=("parallel", "arbitrary"), vmem_limit_bytes=VMEM_LIMIT),
        name="gdn_core",
    )(proj, proj, proj, proj, proj, conv_w, gate_par, o_norm_w.reshape(1, HEAD_DIM), cs)


def _sb_kernel(q_ref, k_ref, v_ref, gate_ref, mt_ref, o_ref, vt_ref, acc_ref, c_ref, za_ref, pb_ref):
    tq = q_ref.shape[1]
    seq = k_ref.shape[1]
    qi = pl.program_id(2)
    n_sub = tq // SB_SUB

    @pl.when(qi == 0)
    def _():
        def xpose(i, carry):
            r0 = pl.multiple_of(i * tq, tq)
            vt_ref[:, pl.ds(r0, tq)] = v_ref[0, pl.ds(r0, tq), :].astype(F32).T.astype(BF16)
            return carry
        lax.fori_loop(0, seq // tq, xpose, 0)

    q = q_ref[0]
    mt = mt_ref[...]
    acc_ref[...] = jnp.zeros_like(acc_ref)
    c_ref[...] = jnp.zeros_like(c_ref)

    def probabilities(z_sub, c, masked):
        probs = [None] * n_sub
        for jj in reversed(range(n_sub)):
            q0 = jj * SB_SUB if masked else 0
            zj = z_sub(jj)[:, q0:]
            cj = c[:, q0:]
            sp = jnp.maximum(zj, 0.0) + jnp.log(1.0 + jnp.exp2(-jnp.abs(zj))) * LOG2E
            if masked:
                mask = (lax.broadcasted_iota(jnp.int32, zj.shape, 0)
                        < lax.broadcasted_iota(jnp.int32, zj.shape, 1))
                sp = jnp.where(mask, sp, 0.0)
            if masked:
                t = _dot(mt, sp.astype(BF16))
                a = jnp.where(mask, jnp.exp2(zj - sp - t - cj), 0.0).astype(BF16)
                cj = cj + (t[0:1, :] + sp[0:1, :])
            else:
                half = zj.shape[1] // 2
                a_parts, c_parts = [], []
                for lo in (0, half):
                    cols = slice(lo, lo + half)
                    sp_h = sp[:, cols]
                    t = _dot(mt, sp_h.astype(BF16))
                    a_parts.append(jnp.exp2(zj[:, cols] - sp_h - t - cj[:, cols]).astype(BF16))
                    c_parts.append(cj[:, cols] + (t[0:1, :] + sp_h[0:1, :]))
                a = jnp.concatenate(a_parts, axis=1)
                cj = jnp.concatenate(c_parts, axis=1)
            if q0:
                a = jnp.concatenate([jnp.zeros((SB_SUB, q0), BF16), a], axis=1)
                cj = jnp.concatenate([c[:, :q0], cj], axis=1)
            probs[jj] = a
            c = cj
        return jnp.concatenate(probs, axis=0), c

    def start_of(g):
        return pl.multiple_of(g * tq, tq)

    def scores(g):
        return _dot_nt(k_ref[0, pl.ds(start_of(g), tq), :], q)

    def weighted_values(g, probs):
        return _dot(vt_ref[:, pl.ds(start_of(g), tq)], probs)

    def sub_of(z):
        return lambda jj: z[jj * SB_SUB:(jj + 1) * SB_SUB, :]

    odd = qi % 2
    n_pairs = qi // 2
    first_a = jnp.maximum(qi - 1 - odd, 0)

    def prologue(groups):
        c = c_ref[...]
        z = scores(groups[0][0])
        za_ref[...] = scores(first_a)
        for i, (g, masked) in enumerate(groups):
            probs, c = probabilities(sub_of(z), c, masked)
            if i + 1 < len(groups):
                z = scores(groups[i + 1][0])
                acc_ref[...] += weighted_values(g, probs)
            else:
                pb_ref[...] = probs
        c_ref[...] = c

    @pl.when(odd == 0)
    def _():
        prologue([(qi, True)])

    @pl.when(odd == 1)
    def _():
        prologue([(qi, True), (qi - 1, False)])

    def body(it, carry):
        ga = qi - 1 - odd - 2 * it
        gb = ga - 1
        av_prev = weighted_values(ga + 1, pb_ref[...])
        probs_a, c = probabilities(lambda jj: za_ref[jj * SB_SUB:(jj + 1) * SB_SUB, :], c_ref[...], False)
        zb = scores(gb)
        av_a = weighted_values(ga, probs_a)
        za_ref[...] = scores(jnp.maximum(ga - 2, 0))
        probs_b, c = probabilities(sub_of(zb), c, False)
        pb_ref[...] = probs_b
        c_ref[...] = c
        acc_ref[...] += av_prev + av_a
        return carry

    lax.fori_loop(0, n_pairs, body, 0)
    acc_ref[...] += weighted_values(0, pb_ref[...])
    gate = gate_ref[0]
    o_ref[0] = acc_ref[...].T * (gate * _sigmoid(gate))


def _sb_attention(qkv, gate, mo, bsz, seq):
    tq = SB_BLOCK
    return pl.pallas_call(
        _sb_kernel,
        grid=(bsz, HEADS, seq // tq),
        in_specs=[
            pl.BlockSpec((1, tq, HEAD_DIM), lambda b, h, i: (b, i, h)),
            pl.BlockSpec((1, seq, HEAD_DIM), lambda b, h, i: (b, 0, HEADS + h)),
            pl.BlockSpec((1, seq, HEAD_DIM), lambda b, h, i: (b, 0, 2 * HEADS + h)),
            pl.BlockSpec((1, tq, HEAD_DIM), lambda b, h, i: (b, i, h)),
            pl.BlockSpec((SB_SUB, SB_SUB), lambda b, h, i: (0, 0)),
        ],
        out_specs=pl.BlockSpec((1, tq, HEAD_DIM), lambda b, h, i: (b, i, h)),
        out_shape=jax.ShapeDtypeStruct((bsz, seq, WIDTH), F32),
        scratch_shapes=[
            pltpu.VMEM((HEAD_DIM, seq), BF16),
            pltpu.VMEM((HEAD_DIM, tq), F32),
            pltpu.VMEM((1, tq), F32),
            pltpu.VMEM((tq, tq), F32),
            pltpu.VMEM((tq, tq), BF16),
        ],
        compiler_params=pltpu.CompilerParams(
            dimension_semantics=("parallel", "parallel", "arbitrary"), vmem_limit_bytes=VMEM_LIMIT),
        name="sb_attention",
    )(qkv, qkv, qkv, gate, mo)


def _cumsum_matrix(tb):
    i = jnp.arange(tb)
    same_chunk = (i[:, None] // CHUNK) == (i[None, :] // CHUNK)
    return jnp.logical_and(same_chunk, i[:, None] >= i[None, :]).astype(BF16)


def _tail_matrix():
    s = jnp.arange(SB_SUB)[:, None]
    j = jnp.arange(SB_SUB)[None, :]
    return (j > s).astype(BF16)


def kernel(x, norm_w, a_w_in, a_conv_w, a_a_log, a_dt_bias, a_o_norm, a_w_out, b_w_in, b_w_out, final_norm_w):
    bsz, seq, d = x.shape
    m = bsz * seq
    depth = norm_w.shape[0]
    tm = min(1024, m)
    h2 = x.reshape(m, d)
    for i in range(depth):
        j = i // 2
        last = i == depth - 1
        if i % 2 == 0:
            wa = a_w_in[j]
            pad = jnp.zeros((d, LANES - 2 * HEADS), F32)
            w_all = jnp.concatenate([wa, pad], axis=1).astype(BF16)
            proj = _norm_proj(h2, norm_w[i], w_all, F32, tm, w_all.shape[1] // 3)
            gate_par = jnp.zeros((2, LANES), F32)
            gate_par = gate_par.at[0, HEADS:2 * HEADS].set(a_a_log[j])
            gate_par = gate_par.at[1, HEADS:2 * HEADS].set(a_dt_bias[j])
            og = _gdn_core(proj.reshape(bsz, seq, -1), a_conv_w[j], gate_par, a_o_norm[j],
                           _cumsum_matrix(GDN_BLOCK), bsz, seq)
            w_out = a_w_out[j]
        else:
            wb = b_w_in[j]
            w_qkv = jnp.concatenate([wb[:, :WIDTH] * (HEAD_DIM ** -0.5 * LOG2E), wb[:, WIDTH:3 * WIDTH]], axis=1)
            qkv = _norm_proj(h2, norm_w[i], w_qkv.astype(BF16), BF16, tm, WIDTH)
            gate = _norm_proj(h2, norm_w[i], wb[:, 3 * WIDTH:].astype(BF16), F32, tm, WIDTH)
            og = _sb_attention(qkv.reshape(bsz, seq, -1), gate.reshape(bsz, seq, -1),
                               _tail_matrix(), bsz, seq)
            w_out = b_w_out[j]
        h2 = _out_proj(og.reshape(m, WIDTH), w_out.astype(BF16), h2, final_norm_w, last, min(512, m))
    return h2.reshape(bsz, seq, d)
```
